```python
import math
import jax, jax.numpy as jnp
from jax import lax
import numpy as np

D_MODEL = 2048
BATCH = 4
SEQ = 4096
DEPTH = 4

HEAD_DIM = 128
MIX_WIDTH = D_MODEL
A_HEADS = 4
A_NOPE = 128
A_ROPE = 64
A_V = 128
Q_LORA_RANK = 448
KV_LORA_RANK = 512
B_HEADS = 6
B_KV_HEADS = 2
B_GROUP = B_HEADS // B_KV_HEADS
GRID_W = 64
C_HEADS = 6
C_BRANCHES = ((128, 1), (512, 4), (2048, 16))
D_FF = 4 * D_MODEL
ROPE_THETA = 10000.0
Q_BLOCK = 128
EPS = 1e-6
NEG_INF = -1e30
IN_SIZES = (Q_LORA_RANK, KV_LORA_RANK, A_ROPE,
            B_HEADS * HEAD_DIM, B_KV_HEADS * HEAD_DIM, B_KV_HEADS * HEAD_DIM,
            C_HEADS * HEAD_DIM, C_HEADS * HEAD_DIM, C_HEADS * HEAD_DIM)
IN_WIDTH = sum(IN_SIZES)
OUT_SIZES = (A_HEADS * A_V, B_HEADS * HEAD_DIM, C_HEADS * HEAD_DIM)

kernel_name = "hymba_style_mla_gqa2d_dilated_encoder"


def _rms(x, g=None):
    xf = x.astype(jnp.float32)
    y = xf * lax.rsqrt(jnp.mean(xf * xf, axis=-1, keepdims=True) + EPS)
    if g is not None:
        y = y * g.astype(jnp.float32)
    return y.astype(x.dtype)


def _rope_angles(pos, dim):
    inv = jnp.power(ROPE_THETA, -jnp.arange(0, dim, 2, dtype=jnp.float32) / dim)
    ang = pos.astype(jnp.float32)[:, None] * inv[None, :]
    return jnp.cos(ang), jnp.sin(ang)


def _apply_rope(x, cs):
    cos, sin = cs
    cos = cos[:, None, :]
    sin = sin[:, None, :]
    xf = x.astype(jnp.float32)
    half = x.shape[-1] // 2
    x1, x2 = xf[..., :half], xf[..., half:]
    return jnp.concatenate([x1 * cos - x2 * sin, x1 * sin + x2 * cos], axis=-1).astype(x.dtype)


def _split_cols(a, sizes):
    out = []
    start = 0
    for s in sizes:
        out.append(a[..., start:start + s])
        start += s
    return out


def _dense_attn_blocked(q, k, v, scale):
    B, S, Hkv, G, Dk = q.shape
    nb = S // Q_BLOCK
    qb = q.reshape(B, nb, Q_BLOCK, Hkv, G, Dk).transpose(1, 0, 2, 3, 4, 5)

    def one_block(qblk):
        s = jnp.einsum('bqhgd,bkhd->bhgqk', qblk, k).astype(jnp.float32) * scale
        p = jax.nn.softmax(s, axis=-1)
        return jnp.einsum('bhgqk,bkhd->bqhgd', p.astype(v.dtype), v)

    ob = lax.map(one_block, qb)
    return ob.transpose(1, 0, 2, 3, 4, 5).reshape(B, S, Hkv, G, v.shape[-1])


def _band_attn(q, k, v, half, scale):
    N, L, H, D = q.shape
    blk = half
    nb = -(-L // blk)
    pad = nb * blk - L
    qp = jnp.pad(q, ((0, 0), (0, pad), (0, 0), (0, 0)))
    kp = jnp.pad(k, ((0, 0), (blk, pad + blk), (0, 0), (0, 0))).reshape(N, nb + 2, blk, H, D)
    vp = jnp.pad(v, ((0, 0), (blk, pad + blk), (0, 0), (0, 0))).reshape(N, nb + 2, blk, H, D)
    kwin = jnp.concatenate([kp[:, :-2], kp[:, 1:-1], kp[:, 2:]], axis=2)
    vwin = jnp.concatenate([vp[:, :-2], vp[:, 1:-1], vp[:, 2:]], axis=2)
    qb = qp.reshape(N, nb, blk, H, D)
    s = jnp.einsum('nbqhd,nbkhd->nbhqk', qb, kwin).astype(jnp.float32) * scale
    qpos = jnp.arange(nb)[:, None] * blk + jnp.arange(blk)[None, :]
    kpos = (jnp.arange(nb)[:, None] - 1) * blk + jnp.arange(3 * blk)[None, :]
    rel = kpos[:, None, :] - qpos[:, :, None]
    mask = (jnp.abs(rel) <= half) & (kpos[:, None, :] >= 0) & (kpos[:, None, :] < L)
    s = jnp.where(mask[None, :, None, :, :], s, NEG_INF)
    m = jnp.max(s, axis=-1, keepdims=True)
    e = jnp.exp(s - m)
    den = jnp.sum(e, axis=-1, keepdims=True)
    o = jnp.einsum('nbhqk,nbkhd->nbqhd', (e / den).astype(v.dtype), vwin)
    lse = (m + jnp.log(den))[..., 0]
    o = o.reshape(N, nb * blk, H, D)[:, :L]
    lse = lse.transpose(0, 1, 3, 2).reshape(N, nb * blk, H)[:, :L]
    return o, lse


def _dilated_mixture(q, k, v, scale):
    B, S, H, D = q.shape
    outs = []
    lses = []
    for window, dil in C_BRANCHES:
        half = window // (2 * dil)
        L = S // dil

        def to_sub(t):
            return t.reshape(B, L, dil, H, D).transpose(0, 2, 1, 3, 4).reshape(B * dil, L, H, D)

        o, lse = _band_attn(to_sub(q), to_sub(k), to_sub(v), half, scale)
        outs.append(o.reshape(B, dil, L, H, D).transpose(0, 2, 1, 3, 4).reshape(B, S, H, D))
        lses.append(lse.reshape(B, dil, L, H).transpose(0, 2, 1, 3).reshape(B, S, H))
    w = jax.nn.softmax(jnp.stack(lses, axis=0), axis=0)
    o = jnp.stack(outs, axis=0).astype(jnp.float32)
    return jnp.sum(w[..., None] * o, axis=0).astype(q.dtype)


def setup_inputs(seed: int = 0) -> dict:
    key = jax.random.key(seed)
    ks = jax.random.split(key, 16)
    f32 = jnp.float32

    def nrm(k, shape, scale):
        return jax.random.normal(k, shape, f32) * scale

    def gain(k, shape):
        return 1.0 + 0.02 * jax.random.normal(k, shape, f32)

    return {
        "x": jax.random.normal(ks[0], (BATCH, SEQ, D_MODEL), f32),
        "ln1_g": gain(ks[1], (DEPTH, D_MODEL)),
        "w_in": nrm(ks[2], (DEPTH, D_MODEL, IN_WIDTH), D_MODEL ** -0.5),
        "g_q_a": gain(ks[3], (DEPTH, Q_LORA_RANK)),
        "w_uq": nrm(ks[4], (DEPTH, Q_LORA_RANK, A_HEADS * (A_NOPE + A_ROPE)), Q_LORA_RANK ** -0.5),
        "g_kv_a": gain(ks[5], (DEPTH, KV_LORA_RANK)),
        "w_ukv": nrm(ks[6], (DEPTH, KV_LORA_RANK, A_HEADS * (A_NOPE + A_V)), KV_LORA_RANK ** -0.5),
        "g_qn_b": gain(ks[7], (DEPTH, HEAD_DIM)),
        "g_kn_b": gain(ks[8], (DEPTH, HEAD_DIM)),
        "g_out": gain(ks[9], (DEPTH, MIX_WIDTH)),
        "w_out": nrm(ks[10], (DEPTH, MIX_WIDTH, D_MODEL), MIX_WIDTH ** -0.5),
        "ln2_g": gain(ks[11], (DEPTH, D_MODEL)),
        "w_ff1": nrm(ks[12], (DEPTH, D_MODEL, D_FF), D_MODEL ** -0.5),
        "w_ff2": nrm(ks[13], (DEPTH, D_FF, D_MODEL), D_FF ** -0.5),
        "ln_f_g": gain(ks[14], (D_MODEL,)),
    }


def reference(x, ln1_g, w_in, g_q_a, w_uq, g_kv_a, w_ukv, g_qn_b, g_kn_b, g_out, w_out,
              ln2_g, w_ff1, w_ff2, ln_f_g):
    B, S, _ = x.shape
    ROWS = S // GRID_W
    pos = jnp.arange(S, dtype=jnp.float32)
    row = jnp.repeat(jnp.arange(ROWS, dtype=jnp.float32), GRID_W)
    col = jnp.tile(jnp.arange(GRID_W, dtype=jnp.float32), ROWS)
    cs_a = _rope_angles(pos, A_ROPE)
    cs_c = _rope_angles(pos, HEAD_DIM)
    cs_row = _rope_angles(row, HEAD_DIM // 2)
    cs_col = _rope_angles(col, HEAD_DIM // 2)
    scale_a = 1.0 / math.sqrt(A_NOPE + A_ROPE)
    scale_h = 1.0 / math.sqrt(HEAD_DIM)

    def axial(t):
        hd = HEAD_DIM // 2
        return jnp.concatenate([_apply_rope(t[..., :hd], cs_row), _apply_rope(t[..., hd:], cs_col)], axis=-1)

    for l in range(DEPTH):
        h = _rms(x, ln1_g[l])
        proj = h @ w_in[l]
        a_cq, a_ckv, a_kr, b_q, b_k, b_v, c_q, c_k, c_v = _split_cols(proj, IN_SIZES)

        qa = (_rms(a_cq, g_q_a[l]) @ w_uq[l]).reshape(B, S, A_HEADS, A_NOPE + A_ROPE)
        qa = jnp.concatenate([qa[..., :A_NOPE], _apply_rope(qa[..., A_NOPE:], cs_a)], axis=-1)
        kva = (_rms(a_ckv, g_kv_a[l]) @ w_ukv[l]).reshape(B, S, A_HEADS, A_NOPE + A_V)
        k_pe = _apply_rope(a_kr[:, :, None, :], cs_a)
        ka = jnp.concatenate([kva[..., :A_NOPE], jnp.broadcast_to(k_pe, (B, S, A_HEADS, A_ROPE))], axis=-1)
        va = kva[..., A_NOPE:]
        o_a = _dense_attn_blocked(qa[:, :, :, None, :], ka, va, scale_a).reshape(B, S, OUT_SIZES[0])

        qb = axial(_rms(b_q.reshape(B, S, B_HEADS, HEAD_DIM), g_qn_b[l]))
        kb = axial(_rms(b_k.reshape(B, S, B_KV_HEADS, HEAD_DIM), g_kn_b[l]))
        vb = b_v.reshape(B, S, B_KV_HEADS, HEAD_DIM)
        qb = qb.reshape(B, S, B_KV_HEADS, B_GROUP, HEAD_DIM)
        o_b = _dense_attn_blocked(qb, kb, vb, scale_h).reshape(B, S, OUT_SIZES[1])

        qc = _apply_rope(c_q.reshape(B, S, C_HEADS, HEAD_DIM), cs_c)
        kc = _apply_rope(c_k.reshape(B, S, C_HEADS, HEAD_DIM), cs_c)
        vc = c_v.reshape(B, S, C_HEADS, HEAD_DIM)
        o_c = _dilated_mixture(qc, kc, vc, scale_h).reshape(B, S, OUT_SIZES[2])

        mixed = jnp.concatenate([_rms(o_a), _rms(o_b), _rms(o_c)], axis=-1) * g_out[l]
        x = x + mixed @ w_out[l]

        u = jnp.square(jax.nn.relu(_rms(x, ln2_g[l]) @ w_ff1[l]))
        x = x + u @ w_ff2[l]

    return _rms(x, ln_f_g)
```

```python
import functools
import math

import jax
import jax.numpy as jnp
import numpy as np
from jax import lax
from jax.experimental import pallas as pl
from jax.experimental.pallas import tpu as pltpu

F32 = jnp.float32
BF16 = jnp.bfloat16

D_MODEL = 2048
HEAD_DIM = 128
A_HEADS, A_NOPE, A_ROPE, A_V = 4, 128, 64, 128
Q_LORA, KV_LORA = 448, 512
Q_LORA_PAD = 512
B_HEADS, B_KV_HEADS = 6, 2
B_GROUP = B_HEADS // B_KV_HEADS
GRID_W = 64
C_HEADS = 6
C_BRANCHES = ((128, 1), (512, 4), (2048, 16))
D_FF = 4 * D_MODEL
ROPE_THETA = 10000.0
EPS = 1e-6
MASK_VALUE = -1e30
SCALE_A = 1.0 / math.sqrt(A_NOPE + A_ROPE)
SCALE_H = 1.0 / math.sqrt(HEAD_DIM)

A_QK = 256
C_BLK = 256
C_REACH = 4
VMEM_LIMIT = 56 * 2**20


def _params(n_axes):
    return pltpu.CompilerParams(dimension_semantics=("arbitrary",) * n_axes,
                                vmem_limit_bytes=VMEM_LIMIT)


def _rms(t, n=None):
    n = t.shape[-1] if n is None else n
    return t * lax.rsqrt(jnp.sum(t * t, axis=-1, keepdims=True) * (1.0 / n) + EPS)


def _rope(t, ct, st):
    return t * ct + pltpu.roll(t, 64, 1) * st


def _rmsnorm_body(x_ref, g_ref, o_ref):
    o_ref[...] = (_rms(x_ref[...]) * g_ref[...]).astype(o_ref.dtype)


def _rmsnorm(x, g, out_dtype, tm=512):
    T, W = x.shape
    return pl.pallas_call(
        _rmsnorm_body,
        grid=(T // tm,),
        in_specs=[pl.BlockSpec((tm, W), lambda i: (i, 0)),
                  pl.BlockSpec((1, W), lambda i: (0, 0))],
        out_specs=pl.BlockSpec((tm, W), lambda i: (i, 0)),
        out_shape=jax.ShapeDtypeStruct((T, W), out_dtype),
        compiler_params=_params(1),
        name="rmsnorm",
    )(x, g)


def _proj_a_body(h_ref, w_ref, gq_ref, gkv_ref, wuq_ref, wukv_ref, ct_ref, st_ref,
                 qa_ref, ka_ref, va_ref):
    pa = jnp.dot(h_ref[...], w_ref[...], preferred_element_type=F32)
    cq = pa[:, :Q_LORA_PAD]
    ckv = pa[:, Q_LORA_PAD:Q_LORA_PAD + KV_LORA]
    kr = pa[:, Q_LORA_PAD + KV_LORA:]
    cqn = (_rms(cq, Q_LORA) * gq_ref[...]).astype(BF16)
    ckvn = (_rms(ckv) * gkv_ref[...]).astype(BF16)
    ct = ct_ref[...]
    st = st_ref[...]
    kpe = _rope(kr, ct, st).astype(BF16)
    qa = jnp.dot(cqn, wuq_ref[...], preferred_element_type=F32)
    kva = jnp.dot(ckvn, wukv_ref[...], preferred_element_type=F32)
    for hh in range(A_HEADS):
        c0 = hh * A_QK
        qa_ref[:, c0:c0 + A_NOPE] = (qa[:, c0:c0 + A_NOPE] * SCALE_A).astype(BF16)
        qa_ref[:, c0 + A_NOPE:c0 + A_QK] = (
            _rope(qa[:, c0 + A_NOPE:c0 + A_QK], ct, st) * SCALE_A).astype(BF16)
        ka_ref[:, c0:c0 + A_NOPE] = kva[:, hh * A_NOPE:(hh + 1) * A_NOPE].astype(BF16)
        ka_ref[:, c0 + A_NOPE:c0 + A_QK] = kpe
    va_ref[...] = kva[:, A_HEADS * A_NOPE:].astype(BF16)


def _proj_b_body(h_ref, w_ref, gq_ref, gk_ref, ct_ref, st_ref, q_ref, k_ref, v_ref):
    pb = jnp.dot(h_ref[...], w_ref[...], preferred_element_type=F32)
    ct = ct_ref[...]
    st = st_ref[...]
    for i in range(B_HEADS):
        t = _rms(pb[:, i * HEAD_DIM:(i + 1) * HEAD_DIM]) * gq_ref[...]
        q_ref[:, i * HEAD_DIM:(i + 1) * HEAD_DIM] = (_rope(t, ct, st) * SCALE_H).astype(BF16)
    k0 = B_HEADS * HEAD_DIM
    for i in range(B_KV_HEADS):
        t = _rms(pb[:, k0 + i * HEAD_DIM:k0 + (i + 1) * HEAD_DIM]) * gk_ref[...]
        k_ref[:, i * HEAD_DIM:(i + 1) * HEAD_DIM] = _rope(t, ct, st).astype(BF16)
    v_ref[...] = pb[:, k0 + B_KV_HEADS * HEAD_DIM:].astype(BF16)


def _proj_c_body(h_ref, w_ref, ct_ref, st_ref, q_ref, k_ref, v_ref):
    pc = jnp.dot(h_ref[...], w_ref[...], preferred_element_type=F32)
    ct = ct_ref[...]
    st = st_ref[...]
    W = C_HEADS * HEAD_DIM
    for i in range(C_HEADS):
        sl = slice(i * HEAD_DIM, (i + 1) * HEAD_DIM)
        q_ref[:, sl] = (_rope(pc[:, sl], ct, st) * SCALE_H).astype(BF16)
        k_ref[:, sl] = _rope(pc[:, W + i * HEAD_DIM:W + (i + 1) * HEAD_DIM], ct, st).astype(BF16)
    v_ref[...] = pc[:, 2 * W:].astype(BF16)


def _row_spec(tm, w):
    return pl.BlockSpec((tm, w), lambda i: (i, 0))


def _const_spec(shape):
    return pl.BlockSpec(shape, lambda i: (0,) * len(shape))


def _proj_call(body, name, h, consts, tables, out_widths, seq, tm=512):
    T = h.shape[0]
    nseq = seq // tm
    tab_spec = pl.BlockSpec((tm, HEAD_DIM), lambda i: (i % nseq, 0))
    return pl.pallas_call(
        body,
        grid=(T // tm,),
        in_specs=([_row_spec(tm, h.shape[1])] + [_const_spec(c.shape) for c in consts]
                  + [tab_spec] * len(tables)),
        out_specs=[_row_spec(tm, w) for w in out_widths],
        out_shape=[jax.ShapeDtypeStruct((T, w), BF16) for w in out_widths],
        compiler_params=_params(1),
        name=name,
    )(h, *consts, *tables)


def _attn_body(*refs, groups, dk, dv, tq, tk, nk, reach):
    if reach is None:
        q_ref, k_ref, v_ref, o_ref = refs
        bias_ref = None
    else:
        q_ref, k_ref, v_ref, bias_ref, o_ref = refs
    qi = pl.program_id(2)
    q = jnp.concatenate([q_ref[0, :, g * dk:(g + 1) * dk] for g in range(groups)], axis=0)
    M = groups * tq

    def step(j, carry):
        m, l, acc = carry
        start = pl.multiple_of(j * tk, tk)
        kj = k_ref[0, pl.ds(start, tk), :]
        vj = v_ref[0, pl.ds(start, tk), :]
        s = lax.dot_general(q, kj, (((1,), (1,)), ((), ())), preferred_element_type=F32)
        if bias_ref is not None:
            s = s + bias_ref[j - qi + reach]
        m_new = jnp.maximum(m, jnp.max(s, axis=1, keepdims=True))
        alpha = jnp.exp(m - m_new)
        p = jnp.exp(s - m_new)
        l = alpha * l + jnp.sum(p, axis=1, keepdims=True)
        acc = alpha * acc + jnp.dot(p.astype(BF16), vj, preferred_element_type=F32)
        return m_new, l, acc

    init = (jnp.full((M, 1), MASK_VALUE, F32), jnp.zeros((M, 1), F32), jnp.zeros((M, dv), F32))
    if reach is None:
        lo, hi = 0, nk
    else:
        lo, hi = jnp.maximum(qi - reach, 0), jnp.minimum(qi + reach + 1, nk)
    _, l, acc = lax.fori_loop(lo, hi, step, init)
    o = acc / l
    for g in range(groups):
        o_ref[0, :, g * dv:(g + 1) * dv] = o[g * tq:(g + 1) * tq].astype(o_ref.dtype)


def _attention(q, k, v, *, kv_heads, groups, dk, dv, tq, tk, bias=None, reach=None, name):
    B, S, _ = q.shape
    nk = S // tk
    in_specs = [pl.BlockSpec((1, tq, groups * dk), lambda b, h, i: (b, i, h)),
                pl.BlockSpec((1, S, dk), lambda b, h, i: (b, 0, h)),
                pl.BlockSpec((1, S, dv), lambda b, h, i: (b, 0, h))]
    args = [q, k, v]
    if bias is not None:
        in_specs.append(pl.BlockSpec(bias.shape, lambda b, h, i: (0, 0, 0)))
        args.append(bias)
    return pl.pallas_call(
        functools.partial(_attn_body, groups=groups, dk=dk, dv=dv, tq=tq, tk=tk, nk=nk, reach=reach),
        grid=(B, kv_heads, S // tq),
        in_specs=in_specs,
        out_specs=pl.BlockSpec((1, tq, groups * dv), lambda b, h, i: (b, i, h)),
        out_shape=jax.ShapeDtypeStruct((B, S, kv_heads * groups * dv), BF16),
        compiler_params=_params(3),
        name=name,
    )(*args)


def _dilated_bias():
    off = (np.arange(-C_REACH, C_REACH + 1)[:, None, None] * C_BLK
           + np.arange(C_BLK)[None, None, :] - np.arange(C_BLK)[None, :, None])
    count = np.zeros(off.shape, np.float64)
    for window, dil in C_BRANCHES:
        count += (off % dil == 0) & (np.abs(off) <= (window // (2 * dil)) * dil)
    return jnp.asarray(np.where(count > 0, np.log(np.maximum(count, 1.0)), MASK_VALUE), F32)


def _out_body(oa_ref, ob_ref, oc_ref, g_ref, w_ref, x_ref, g2_ref, xo_ref, h2_ref):
    mixed = jnp.concatenate([_rms(r[...].astype(F32)) for r in (oa_ref, ob_ref, oc_ref)], axis=1)
    mixed = (mixed * g_ref[...]).astype(BF16)
    y = x_ref[...] + jnp.dot(mixed, w_ref[...], preferred_element_type=F32)
    xo_ref[...] = y
    h2_ref[...] = (_rms(y) * g2_ref[...]).astype(BF16)


def _out_proj(oa, ob, oc, g_out, w_out, x, ln2_g, tm=512):
    T = x.shape[0]
    return pl.pallas_call(
        _out_body,
        grid=(T // tm,),
        in_specs=[_row_spec(tm, oa.shape[1]), _row_spec(tm, ob.shape[1]), _row_spec(tm, oc.shape[1]),
                  _const_spec(g_out.shape), _const_spec(w_out.shape), _row_spec(tm, D_MODEL),
                  _const_spec(ln2_g.shape)],
        out_specs=[_row_spec(tm, D_MODEL), _row_spec(tm, D_MODEL)],
        out_shape=[jax.ShapeDtypeStruct((T, D_MODEL), F32), jax.ShapeDtypeStruct((T, D_MODEL), BF16)],
        compiler_params=_params(1),
        name="out_proj",
    )(oa, ob, oc, g_out, w_out, x, ln2_g)


def _ff1_body(a_ref, w_ref, o_ref):
    acc = jnp.dot(a_ref[...], w_ref[...], preferred_element_type=F32)
    o_ref[...] = jnp.square(jnp.maximum(acc, 0.0)).astype(o_ref.dtype)


def _ff1(a, w, tm=1024, tn=1024):
    M, K = a.shape
    N = w.shape[1]
    return pl.pallas_call(
        _ff1_body,
        grid=(M // tm, N // tn),
        in_specs=[pl.BlockSpec((tm, K), lambda i, j: (i, 0)),
                  pl.BlockSpec((K, tn), lambda i, j: (0, j))],
        out_specs=pl.BlockSpec((tm, tn), lambda i, j: (i, j)),
        out_shape=jax.ShapeDtypeStruct((M, N), BF16),
        compiler_params=_params(2),
        name="ff1",
    )(a, w)


def _ff2_body(a_ref, w_ref, x_ref, o_ref, acc_ref):
    k = pl.program_id(2)

    @pl.when(k == 0)
    def _():
        acc_ref[...] = x_ref[...]

    acc_ref[...] += jnp.dot(a_ref[...], w_ref[...], preferred_element_type=F32)

    @pl.when(k == pl.num_programs(2) - 1)
    def _():
        o_ref[...] = acc_ref[...]


def _ff2(a, w, x, tm=1024, tn=1024, tk=2048):
    M, K = a.shape
    N = w.shape[1]
    return pl.pallas_call(
        _ff2_body,
        grid=(M // tm, N // tn, K // tk),
        in_specs=[pl.BlockSpec((tm, tk), lambda i, j, k: (i, k)),
                  pl.BlockSpec((tk, tn), lambda i, j, k: (k, j)),
                  pl.BlockSpec((tm, tn), lambda i, j, k: (i, j))],
        out_specs=pl.BlockSpec((tm, tn), lambda i, j, k: (i, j)),
        out_shape=jax.ShapeDtypeStruct((M, N), F32),
        scratch_shapes=[pltpu.VMEM((tm, tn), F32)],
        compiler_params=_params(3),
        name="ff2",
    )(a, w, x)


def _rope_tables(S):
    def angles(pos, dim):
        inv = jnp.power(ROPE_THETA, -jnp.arange(0, dim, 2, dtype=F32) / dim)
        ang = pos.astype(F32)[:, None] * inv[None, :]
        return jnp.cos(ang), jnp.sin(ang)

    t = jnp.arange(S)
    cos_p, sin_p = angles(t, A_ROPE)
    cos_r, sin_r = angles(t // GRID_W, HEAD_DIM // 2)
    cos_c, sin_c = angles(t % GRID_W, HEAD_DIM // 2)
    cos_f, sin_f = angles(t, HEAD_DIM)
    ta = (jnp.concatenate([cos_p, cos_p, cos_p, cos_p], 1), jnp.concatenate([-sin_p, sin_p, sin_p, sin_p], 1))
    tb = (jnp.concatenate([cos_r, cos_c, cos_r, cos_c], 1), jnp.concatenate([-sin_r, -sin_c, sin_r, sin_c], 1))
    tc = (jnp.concatenate([cos_f, cos_f], 1), jnp.concatenate([-sin_f, sin_f], 1))
    return ta, tb, tc


def _axial_perm():
    q = HEAD_DIM // 4
    return np.concatenate([np.arange(0, q), np.arange(2 * q, 3 * q), np.arange(q, 2 * q), np.arange(3 * q, 4 * q)])


def _spread_rope(w):
    z = jnp.zeros(w.shape[:-1] + (A_ROPE // 2,), w.dtype)
    return jnp.concatenate([w[..., :A_ROPE // 2], z, w[..., A_ROPE // 2:], z], axis=-1)


def _prep_weights(w_in, g_q_a, w_uq, g_kv_a, w_ukv, g_qn_b, g_kn_b):
    L = w_in.shape[0]
    sizes = (Q_LORA, KV_LORA, A_ROPE, B_HEADS * HEAD_DIM, B_KV_HEADS * HEAD_DIM, B_KV_HEADS * HEAD_DIM,
             C_HEADS * HEAD_DIM, C_HEADS * HEAD_DIM, C_HEADS * HEAD_DIM)
    offs = np.concatenate([[0], np.cumsum(sizes)])
    cols = [w_in[:, :, offs[i]:offs[i + 1]] for i in range(len(sizes))]
    perm = _axial_perm()

    def perm_heads(w, n):
        w = w.reshape(L, D_MODEL, n, HEAD_DIM)[..., perm]
        return w.reshape(L, D_MODEL, n * HEAD_DIM)

    w_a = jnp.concatenate([cols[0], jnp.zeros((L, D_MODEL, Q_LORA_PAD - Q_LORA), w_in.dtype),
                           cols[1], _spread_rope(cols[2])], axis=-1).astype(BF16)
    w_b = jnp.concatenate([perm_heads(cols[3], B_HEADS), perm_heads(cols[4], B_KV_HEADS), cols[5]],
                          axis=-1).astype(BF16)
    w_c = jnp.concatenate(cols[6:9], axis=-1).astype(BF16)

    uq = w_uq.reshape(L, Q_LORA, A_HEADS, A_NOPE + A_ROPE)
    uq = jnp.concatenate([uq[..., :A_NOPE], _spread_rope(uq[..., A_NOPE:])], axis=-1)
    uq = uq.reshape(L, Q_LORA, A_HEADS * A_QK)
    uq = jnp.concatenate([uq, jnp.zeros((L, Q_LORA_PAD - Q_LORA, A_HEADS * A_QK), uq.dtype)], axis=1).astype(BF16)
    ukv = w_ukv.reshape(L, KV_LORA, A_HEADS, A_NOPE + A_V)
    ukv = jnp.concatenate([ukv[..., :A_NOPE].reshape(L, KV_LORA, A_HEADS * A_NOPE),
                           ukv[..., A_NOPE:].reshape(L, KV_LORA, A_HEADS * A_V)], axis=-1).astype(BF16)
    gq = jnp.concatenate([g_q_a, jnp.zeros((L, Q_LORA_PAD - Q_LORA), g_q_a.dtype)], axis=-1)
    return dict(w_a=w_a, w_b=w_b, w_c=w_c, uq=uq, ukv=ukv, gq=gq[:, None, :], gkv=g_kv_a[:, None, :],
                gqb=g_qn_b[:, perm][:, None, :], gkb=g_kn_b[:, perm][:, None, :])


def kernel(x, ln1_g, w_in, g_q_a, w_uq, g_kv_a, w_ukv, g_qn_b, g_kn_b, g_out, w_out, ln2_g, w_ff1, w_ff2, ln_f_g):
    B, S, D = x.shape
    T = B * S
    depth = w_in.shape[0]
    p = _prep_weights(w_in, g_q_a, w_uq, g_kv_a, w_ukv, g_qn_b, g_kn_b)
    w_out_b = w_out.astype(BF16)
    w_ff1_b = w_ff1.astype(BF16)
    w_ff2_b = w_ff2.astype(BF16)
    ta, tb, tc = _rope_tables(S)
    bias_c = _dilated_bias()

    xt = x.reshape(T, D)
    for l in range(depth):
        h = _rmsnorm(xt, ln1_g[l][None, :], BF16)
        qa, ka, va = _proj_call(_proj_a_body, "proj_a", h,
                                [p["w_a"][l], p["gq"][l], p["gkv"][l], p["uq"][l], p["ukv"][l]], ta,
                                (A_HEADS * A_QK, A_HEADS * A_QK, A_HEADS * A_V), S)
        qb, kb, vb = _proj_call(_proj_b_body, "proj_b", h, [p["w_b"][l], p["gqb"][l], p["gkb"][l]], tb,
                                (B_HEADS * HEAD_DIM, B_KV_HEADS * HEAD_DIM, B_KV_HEADS * HEAD_DIM), S)
        qc, kc, vc = _proj_call(_proj_c_body, "proj_c", h, [p["w_c"][l]], tc,
                                (C_HEADS * HEAD_DIM,) * 3, S)

        def b3(a):
            return a.reshape(B, S, a.shape[-1])

        o_a = _attention(b3(qa), b3(ka), b3(va), kv_heads=A_HEADS, groups=1, dk=A_QK, dv=A_V,
                         tq=512, tk=512, name="attn_a")
        o_b = _attention(b3(qb), b3(kb), b3(vb), kv_heads=B_KV_HEADS, groups=B_GROUP, dk=HEAD_DIM,
                         dv=HEAD_DIM, tq=256, tk=512, name="attn_b")
        o_c = _attention(b3(qc), b3(kc), b3(vc), kv_heads=C_HEADS, groups=1, dk=HEAD_DIM, dv=HEAD_DIM,
                         tq=C_BLK, tk=C_BLK, bias=bias_c, reach=C_REACH, name="attn_c")

        xt, h2 = _out_proj(o_a.reshape(T, -1), o_b.reshape(T, -1), o_c.reshape(T, -1),
                           g_out[l][None, :], w_out_b[l], xt, ln2_g[l][None, :])
        u = _ff1(h2, w_ff1_b[l])
        xt = _ff2(u, w_ff2_b[l], xt)

    return _rmsnorm(xt, ln_f_g[None, :], F32).reshape(B, S, D)
```

```python
import functools
import math

import jax
import jax.numpy as jnp
import numpy as np
from jax import lax
from jax.experimental import pallas as pl
from jax.experimental.pallas import tpu as pltpu

F32 = jnp.float32
BF16 = jnp.bfloat16

D_MODEL = 2048
HEAD_DIM = 128
A_HEADS, A_NOPE, A_ROPE, A_V = 4, 128, 64, 128
Q_LORA, KV_LORA = 448, 512
Q_LORA_PAD = 512
B_HEADS, B_KV_HEADS = 6, 2
B_GROUP = B_HEADS // B_KV_HEADS
GRID_W = 64
C_HEADS = 6
C_BRANCHES = ((128, 1), (512, 4), (2048, 16))
D_FF = 4 * D_MODEL
ROPE_THETA = 10000.0
EPS = 1e-6
MASK_VALUE = -1e30
LOG2E = math.log2(math.e)
SCALE_A = LOG2E / math.sqrt(A_NOPE + A_ROPE)
SCALE_H = LOG2E / math.sqrt(HEAD_DIM)

A_QK = 256
C_BLK = 256
C_REACH = 4
VMEM_LIMIT = 56 * 2**20


def _params(n_axes):
    return pltpu.CompilerParams(dimension_semantics=("arbitrary",) * n_axes,
                                vmem_limit_bytes=VMEM_LIMIT)


def _rms(t, n=None):
    n = t.shape[-1] if n is None else n
    return t * lax.rsqrt(jnp.sum(t * t, axis=-1, keepdims=True) * (1.0 / n) + EPS)


def _rope(t, ct, st):
    return t * ct + pltpu.roll(t, 64, 1) * st


def _rmsnorm_body(x_ref, g_ref, o_ref):
    o_ref[...] = (_rms(x_ref[...]) * g_ref[...]).astype(o_ref.dtype)


def _rmsnorm(x, g, out_dtype, tm=512):
    T, W = x.shape
    return pl.pallas_call(
        _rmsnorm_body,
        grid=(T // tm,),
        in_specs=[pl.BlockSpec((tm, W), lambda i: (i, 0)),
                  pl.BlockSpec((1, W), lambda i: (0, 0))],
        out_specs=pl.BlockSpec((tm, W), lambda i: (i, 0)),
        out_shape=jax.ShapeDtypeStruct((T, W), out_dtype),
        compiler_params=_params(1),
        name="rmsnorm",
    )(x, g)


def _proj_a_body(h_ref, w_ref, gq_ref, gkv_ref, wuq_ref, wukv_ref, ct_ref, st_ref,
                 qa_ref, ka_ref, vat_ref):
    pa = jnp.dot(h_ref[...], w_ref[...], preferred_element_type=F32)
    cq = pa[:, :Q_LORA_PAD]
    ckv = pa[:, Q_LORA_PAD:Q_LORA_PAD + KV_LORA]
    kr = pa[:, Q_LORA_PAD + KV_LORA:]
    cqn = (_rms(cq, Q_LORA) * gq_ref[...]).astype(BF16)
    ckvn = (_rms(ckv) * gkv_ref[...]).astype(BF16)
    ct = ct_ref[...]
    st = st_ref[...]
    kpe = _rope(kr, ct, st).astype(BF16)
    qa = jnp.dot(cqn, wuq_ref[...], preferred_element_type=F32)
    kva = jnp.dot(ckvn, wukv_ref[...], preferred_element_type=F32)
    for hh in range(A_HEADS):
        c0 = hh * A_QK
        qa_ref[:, c0:c0 + A_NOPE] = (qa[:, c0:c0 + A_NOPE] * SCALE_A).astype(BF16)
        qa_ref[:, c0 + A_NOPE:c0 + A_QK] = (
            _rope(qa[:, c0 + A_NOPE:c0 + A_QK], ct, st) * SCALE_A).astype(BF16)
        ka_ref[:, c0:c0 + A_NOPE] = kva[:, hh * A_NOPE:(hh + 1) * A_NOPE].astype(BF16)
        ka_ref[:, c0 + A_NOPE:c0 + A_QK] = kpe
    vat_ref[0] = kva[:, A_HEADS * A_NOPE:].T.astype(BF16)


def _proj_b_body(h_ref, w_ref, gq_ref, gk_ref, ct_ref, st_ref, q_ref, k_ref, vt_ref):
    pb = jnp.dot(h_ref[...], w_ref[...], preferred_element_type=F32)
    ct = ct_ref[...]
    st = st_ref[...]
    for i in range(B_HEADS):
        t = _rms(pb[:, i * HEAD_DIM:(i + 1) * HEAD_DIM]) * gq_ref[...]
        q_ref[:, i * HEAD_DIM:(i + 1) * HEAD_DIM] = (_rope(t, ct, st) * SCALE_H).astype(BF16)
    k0 = B_HEADS * HEAD_DIM
    for i in range(B_KV_HEADS):
        t = _rms(pb[:, k0 + i * HEAD_DIM:k0 + (i + 1) * HEAD_DIM]) * gk_ref[...]
        k_ref[:, i * HEAD_DIM:(i + 1) * HEAD_DIM] = _rope(t, ct, st).astype(BF16)
    vt_ref[0] = pb[:, k0 + B_KV_HEADS * HEAD_DIM:].T.astype(BF16)


def _proj_c_body(h_ref, w_ref, ct_ref, st_ref, q_ref, k_ref, vt_ref):
    pc = jnp.dot(h_ref[...], w_ref[...], preferred_element_type=F32)
    ct = ct_ref[...]
    st = st_ref[...]
    W = C_HEADS * HEAD_DIM
    for i in range(C_HEADS):
        sl = slice(i * HEAD_DIM, (i + 1) * HEAD_DIM)
        q_ref[:, sl] = (_rope(pc[:, sl], ct, st) * SCALE_H).astype(BF16)
        k_ref[:, sl] = _rope(pc[:, W + i * HEAD_DIM:W + (i + 1) * HEAD_DIM], ct, st).astype(BF16)
    vt_ref[0] = pc[:, 2 * W:].T.astype(BF16)


def _row_spec(tm, w):
    return pl.BlockSpec((tm, w), lambda i: (i, 0))


def _const_spec(shape):
    return pl.BlockSpec(shape, lambda i: (0,) * len(shape))


def _proj_call(body, name, h, consts, tables, row_widths, vt_width, batch, seq, tm=512):
    T = h.shape[0]
    nseq = seq // tm
    tab_spec = pl.BlockSpec((tm, HEAD_DIM), lambda i: (i % nseq, 0))
    return pl.pallas_call(
        body,
        grid=(T // tm,),
        in_specs=([_row_spec(tm, h.shape[1])] + [_const_spec(c.shape) for c in consts]
                  + [tab_spec] * len(tables)),
        out_specs=([_row_spec(tm, w) for w in row_widths]
                   + [pl.BlockSpec((1, vt_width, tm), lambda i: (i // nseq, 0, i % nseq))]),
        out_shape=([jax.ShapeDtypeStruct((T, w), BF16) for w in row_widths]
                   + [jax.ShapeDtypeStruct((batch, vt_width, seq), BF16)]),
        compiler_params=_params(1),
        name=name,
    )(h, *consts, *tables)


def _attn_body(*refs, groups, dk, dv, tq, tk, nk, reach):
    if reach is None:
        q_ref, k_ref, vt_ref, o_ref, s_ref = refs
        chunks = [(c * tk, None) for c in range(nk)]
    else:
        q_ref, k_ref, vt_ref, bias_ref, o_ref, s_ref = refs
        qi = pl.program_id(2)
        chunks = []
        for off in range(-reach, reach + 1):
            j = qi + off
            in_range = jnp.logical_and(j >= 0, j < nk)
            start = pl.multiple_of(jnp.clip(j, 0, nk - 1) * tk, tk)
            chunks.append((pl.ds(start, tk), jnp.where(in_range, off + reach, 2 * reach + 1)))
    q = jnp.concatenate([q_ref[0, :, g * dk:(g + 1) * dk] for g in range(groups)], axis=0)
    N = groups * tq

    def kv_slice(c):
        return slice(c, c + tk) if isinstance(c, int) else c

    m = jnp.full((1, N), MASK_VALUE, F32)
    for c, (kv, bias_idx) in enumerate(chunks):
        s = lax.dot_general(k_ref[0, kv_slice(kv), :], q, (((1,), (1,)), ((), ())),
                            preferred_element_type=F32)
        if bias_idx is not None:
            s = s + bias_ref[bias_idx]
        s_ref[c * tk:(c + 1) * tk, :] = s
        m = jnp.maximum(m, jnp.max(s, axis=0, keepdims=True))

    l = jnp.zeros((1, N), F32)
    acc = jnp.zeros((dv, N), F32)
    for c, (kv, _) in enumerate(chunks):
        p = jnp.exp2(s_ref[c * tk:(c + 1) * tk, :] - m)
        l = l + jnp.sum(p, axis=0, keepdims=True)
        acc = acc + jnp.dot(vt_ref[0, :, kv_slice(kv)], p.astype(BF16), preferred_element_type=F32)
    o_t = acc / l
    for g in range(groups):
        o_ref[0, :, g * dv:(g + 1) * dv] = o_t[:, g * tq:(g + 1) * tq].T.astype(o_ref.dtype)


def _attention(q, k, vt, *, kv_heads, groups, dk, dv, tq, tk, bias=None, reach=None, name):
    B, S, _ = q.shape
    nk = S // tk
    in_specs = [pl.BlockSpec((1, tq, groups * dk), lambda b, h, i: (b, i, h)),
                pl.BlockSpec((1, S, dk), lambda b, h, i: (b, 0, h)),
                pl.BlockSpec((1, dv, S), lambda b, h, i: (b, h, 0))]
    args = [q, k, vt]
    if bias is not None:
        in_specs.append(pl.BlockSpec(bias.shape, lambda b, h, i: (0, 0, 0)))
        args.append(bias)
    n_chunks = nk if reach is None else 2 * reach + 1
    return pl.pallas_call(
        functools.partial(_attn_body, groups=groups, dk=dk, dv=dv, tq=tq, tk=tk, nk=nk, reach=reach),
        grid=(B, kv_heads, S // tq),
        in_specs=in_specs,
        out_specs=pl.BlockSpec((1, tq, groups * dv), lambda b, h, i: (b, i, h)),
        out_shape=jax.ShapeDtypeStruct((B, S, kv_heads * groups * dv), BF16),
        scratch_shapes=[pltpu.VMEM((n_chunks * tk, groups * tq), F32)],
        compiler_params=_params(3),
        name=name,
    )(*args)


def _dilated_bias():
    off = (np.arange(-C_REACH, C_REACH + 1)[:, None, None] * C_BLK
           + np.arange(C_BLK)[None, :, None] - np.arange(C_BLK)[None, None, :])
    count = np.zeros(off.shape, np.float64)
    for window, dil in C_BRANCHES:
        count += (off % dil == 0) & (np.abs(off) <= (window // (2 * dil)) * dil)
    bias = np.where(count > 0, np.log2(np.maximum(count, 1.0)), MASK_VALUE)
    bias = np.concatenate([bias, np.full((1, C_BLK, C_BLK), MASK_VALUE)], axis=0)
    return jnp.asarray(bias, F32)


def _out_body(oa_ref, ob_ref, oc_ref, g_ref, w_ref, x_ref, g2_ref, xo_ref, h2_ref):
    mixed = jnp.concatenate([_rms(r[...].astype(F32)) for r in (oa_ref, ob_ref, oc_ref)], axis=1)
    mixed = (mixed * g_ref[...]).astype(BF16)
    y = x_ref[...] + jnp.dot(mixed, w_ref[...], preferred_element_type=F32)
    xo_ref[...] = y
    h2_ref[...] = (_rms(y) * g2_ref[...]).astype(BF16)


def _out_proj(oa, ob, oc, g_out, w_out, x, ln2_g, tm=512):
    T = x.shape[0]
    return pl.pallas_call(
        _out_body,
        grid=(T // tm,),
        in_specs=[_row_spec(tm, oa.shape[1]), _row_spec(tm, ob.shape[1]), _row_spec(tm, oc.shape[1]),
                  _const_spec(g_out.shape), _const_spec(w_out.shape), _row_spec(tm, D_MODEL),
                  _const_spec(ln2_g.shape)],
        out_specs=[_row_spec(tm, D_MODEL), _row_spec(tm, D_MODEL)],
        out_shape=[jax.ShapeDtypeStruct((T, D_MODEL), F32), jax.ShapeDtypeStruct((T, D_MODEL), BF16)],
        compiler_params=_params(1),
        name="out_proj",
    )(oa, ob, oc, g_out, w_out, x, ln2_g)


def _ff1_body(a_ref, w_ref, o_ref):
    acc = jnp.dot(a_ref[...], w_ref[...], preferred_element_type=F32)
    o_ref[...] = jnp.square(jnp.maximum(acc, 0.0)).astype(o_ref.dtype)


def _ff1(a, w, tm=1024, tn=1024):
    M, K = a.shape
    N = w.shape[1]
    return pl.pallas_call(
        _ff1_body,
        grid=(M // tm, N // tn),
        in_specs=[pl.BlockSpec((tm, K), lambda i, j: (i, 0)),
                  pl.BlockSpec((K, tn), lambda i, j: (0, j))],
        out_specs=pl.BlockSpec((tm, tn), lambda i, j: (i, j)),
        out_shape=jax.ShapeDtypeStruct((M, N), BF16),
        compiler_params=_params(2),
        name="ff1",
    )(a, w)


def _ff2_body(a_ref, w_ref, x_ref, o_ref, acc_ref):
    k = pl.program_id(2)

    @pl.when(k == 0)
    def _():
        acc_ref[...] = x_ref[...]

    acc_ref[...] += jnp.dot(a_ref[...], w_ref[...], preferred_element_type=F32)

    @pl.when(k == pl.num_programs(2) - 1)
    def _():
        o_ref[...] = acc_ref[...]


def _ff2(a, w, x, tm=1024, tn=1024, tk=2048):
    M, K = a.shape
    N = w.shape[1]
    return pl.pallas_call(
        _ff2_body,
        grid=(M // tm, N // tn, K // tk),
        in_specs=[pl.BlockSpec((tm, tk), lambda i, j, k: (i, k)),
                  pl.BlockSpec((tk, tn), lambda i, j, k: (k, j)),
                  pl.BlockSpec((tm, tn), lambda i, j, k: (i, j))],
        out_specs=pl.BlockSpec((tm, tn), lambda i, j, k: (i, j)),
        out_shape=jax.ShapeDtypeStruct((M, N), F32),
        scratch_shapes=[pltpu.VMEM((tm, tn), F32)],
        compiler_params=_params(3),
        name="ff2",
    )(a, w, x)


def _rope_tables(S):
    def angles(pos, dim):
        inv = jnp.power(ROPE_THETA, -jnp.arange(0, dim, 2, dtype=F32) / dim)
        ang = pos.astype(F32)[:, None] * inv[None, :]
        return jnp.cos(ang), jnp.sin(ang)

    t = jnp.arange(S)
    cos_p, sin_p = angles(t, A_ROPE)
    cos_r, sin_r = angles(t // GRID_W, HEAD_DIM // 2)
    cos_c, sin_c = angles(t % GRID_W, HEAD_DIM // 2)
    cos_f, sin_f = angles(t, HEAD_DIM)
    ta = (jnp.concatenate([cos_p, cos_p, cos_p, cos_p], 1), jnp.concatenate([-sin_p, sin_p, sin_p, sin_p], 1))
    tb = (jnp.concatenate([cos_r, cos_c, cos_r, cos_c], 1), jnp.concatenate([-sin_r, -sin_c, sin_r, sin_c], 1))
    tc = (jnp.concatenate([cos_f, cos_f], 1), jnp.concatenate([-sin_f, sin_f], 1))
    return ta, tb, tc


def _axial_perm():
    q = HEAD_DIM // 4
    return np.concatenate([np.arange(0, q), np.arange(2 * q, 3 * q), np.arange(q, 2 * q), np.arange(3 * q, 4 * q)])


def _spread_rope(w):
    z = jnp.zeros(w.shape[:-1] + (A_ROPE // 2,), w.dtype)
    return jnp.concatenate([w[..., :A_ROPE // 2], z, w[..., A_ROPE // 2:], z], axis=-1)


def _prep_weights(w_in, g_q_a, w_uq, g_kv_a, w_ukv, g_qn_b, g_kn_b):
    L = w_in.shape[0]
    sizes = (Q_LORA, KV_LORA, A_ROPE, B_HEADS * HEAD_DIM, B_KV_HEADS * HEAD_DIM, B_KV_HEADS * HEAD_DIM,
             C_HEADS * HEAD_DIM, C_HEADS * HEAD_DIM, C_HEADS * HEAD_DIM)
    offs = np.concatenate([[0], np.cumsum(sizes)])
    cols = [w_in[:, :, offs[i]:offs[i + 1]] for i in range(len(sizes))]
    perm = _axial_perm()

    def perm_heads(w, n):
        w = w.reshape(L, D_MODEL, n, HEAD_DIM)[..., perm]
        return w.reshape(L, D_MODEL, n * HEAD_DIM)

    w_a = jnp.concatenate([cols[0], jnp.zeros((L, D_MODEL, Q_LORA_PAD - Q_LORA), w_in.dtype),
                           cols[1], _spread_rope(cols[2])], axis=-1).astype(BF16)
    w_b = jnp.concatenate([perm_heads(cols[3], B_HEADS), perm_heads(cols[4], B_KV_HEADS), cols[5]],
                          axis=-1).astype(BF16)
    w_c = jnp.concatenate(cols[6:9], axis=-1).astype(BF16)

    uq = w_uq.reshape(L, Q_LORA, A_HEADS, A_NOPE + A_ROPE)
    uq = jnp.concatenate([uq[..., :A_NOPE], _spread_rope(uq[..., A_NOPE:])], axis=-1)
    uq = uq.reshape(L, Q_LORA, A_HEADS * A_QK)
    uq = jnp.concatenate([uq, jnp.zeros((L, Q_LORA_PAD - Q_LORA, A_HEADS * A_QK), uq.dtype)], axis=1).astype(BF16)
    ukv = w_ukv.reshape(L, KV_LORA, A_HEADS, A_NOPE + A_V)
    ukv = jnp.concatenate([ukv[..., :A_NOPE].reshape(L, KV_LORA, A_HEADS * A_NOPE),
                           ukv[..., A_NOPE:].reshape(L, KV_LORA, A_HEADS * A_V)], axis=-1).astype(BF16)
    gq = jnp.concatenate([g_q_a, jnp.zeros((L, Q_LORA_PAD - Q_LORA), g_q_a.dtype)], axis=-1)
    return dict(w_a=w_a, w_b=w_b, w_c=w_c, uq=uq, ukv=ukv, gq=gq[:, None, :], gkv=g_kv_a[:, None, :],
                gqb=g_qn_b[:, perm][:, None, :], gkb=g_kn_b[:, perm][:, None, :])


def kernel(x, ln1_g, w_in, g_q_a, w_uq, g_kv_a, w_ukv, g_qn_b, g_kn_b, g_out, w_out, ln2_g, w_ff1, w_ff2, ln_f_g):
    B, S, D = x.shape
    T = B * S
    depth = w_in.shape[0]
    p = _prep_weights(w_in, g_q_a, w_uq, g_kv_a, w_ukv, g_qn_b, g_kn_b)
    w_out_b = w_out.astype(BF16)
    w_ff1_b = w_ff1.astype(BF16)
    w_ff2_b = w_ff2.astype(BF16)
    ta, tb, tc = _rope_tables(S)
    bias_c = _dilated_bias()

    xt = x.reshape(T, D)
    for l in range(depth):
        h = _rmsnorm(xt, ln1_g[l][None, :], BF16)
        qa, ka, vat = _proj_call(_proj_a_body, "proj_a", h,
                                 [p["w_a"][l], p["gq"][l], p["gkv"][l], p["uq"][l], p["ukv"][l]], ta,
                                 (A_HEADS * A_QK, A_HEADS * A_QK), A_HEADS * A_V, B, S)
        qb, kb, vbt = _proj_call(_proj_b_body, "proj_b", h, [p["w_b"][l], p["gqb"][l], p["gkb"][l]], tb,
                                 (B_HEADS * HEAD_DIM, B_KV_HEADS * HEAD_DIM), B_KV_HEADS * HEAD_DIM, B, S)
        qc, kc, vct = _proj_call(_proj_c_body, "proj_c", h, [p["w_c"][l]], tc,
                                 (C_HEADS * HEAD_DIM,) * 2, C_HEADS * HEAD_DIM, B, S)

        def b3(a):
            return a.reshape(B, S, a.shape[-1])

        o_a = _attention(b3(qa), b3(ka), vat, kv_heads=A_HEADS, groups=1, dk=A_QK, dv=A_V,
                         tq=512, tk=512, name="attn_a")
        o_b = _attention(b3(qb), b3(kb), vbt, kv_heads=B_KV_HEADS, groups=B_GROUP, dk=HEAD_DIM,
                         dv=HEAD_DIM, tq=256, tk=512, name="attn_b")
        o_c = _attention(b3(qc), b3(kc), vct, kv_heads=C_HEADS, groups=1, dk=HEAD_DIM, dv=HEAD_DIM,
                         tq=C_BLK, tk=C_BLK, bias=bias_c, reach=C_REACH, name="attn_c")

        xt, h2 = _out_proj(o_a.reshape(T, -1), o_b.reshape(T, -1), o_c.reshape(T, -1),
                           g_out[l][None, :], w_out_b[l], xt, ln2_g[l][None, :])
        u = _ff1(h2, w_ff1_b[l])
        xt = _ff2(u, w_ff2_b[l], xt)

    return _rmsnorm(xt, ln_f_g[None, :], F32).reshape(B, S, D)
```

```python
import functools
import math

import jax
import jax.numpy as jnp
import numpy as np
from jax import lax
from jax.experimental import pallas as pl
from jax.experimental.pallas import tpu as pltpu

F32 = jnp.float32
BF16 = jnp.bfloat16

D_MODEL = 2048
HEAD_DIM = 128
A_HEADS, A_NOPE, A_ROPE, A_V = 4, 128, 64, 128
Q_LORA, KV_LORA = 448, 512
Q_LORA_PAD = 512
B_HEADS, B_KV_HEADS = 6, 2
B_GROUP = B_HEADS // B_KV_HEADS
GRID_W = 64
C_HEADS = 6
C_BRANCHES = ((128, 1), (512, 4), (2048, 16))
D_FF = 4 * D_MODEL
ROPE_THETA = 10000.0
EPS = 1e-6
MASK_VALUE = -1e30
LOG2E = math.log2(math.e)
SCALE_A = LOG2E / math.sqrt(A_NOPE + A_ROPE)
SCALE_H = LOG2E / math.sqrt(HEAD_DIM)

A_QK = 256
ATT_BLK = 512
C_REACH = 2
VMEM_LIMIT = 56 * 2**20


def _params(n_axes):
    return pltpu.CompilerParams(dimension_semantics=("arbitrary",) * n_axes,
                                vmem_limit_bytes=VMEM_LIMIT)


def _rms(t, n=None):
    n = t.shape[-1] if n is None else n
    return t * lax.rsqrt(jnp.sum(t * t, axis=-1, keepdims=True) * (1.0 / n) + EPS)


def _rope(t, ct, st):
    return t * ct + pltpu.roll(t, 64, 1) * st


def _rmsnorm_body(x_ref, g_ref, o_ref):
    o_ref[...] = (_rms(x_ref[...]) * g_ref[...]).astype(o_ref.dtype)


def _rmsnorm(x, g, out_dtype, tm=512):
    T, W = x.shape
    return pl.pallas_call(
        _rmsnorm_body,
        grid=(T // tm,),
        in_specs=[pl.BlockSpec((tm, W), lambda i: (i, 0)),
                  pl.BlockSpec((1, W), lambda i: (0, 0))],
        out_specs=pl.BlockSpec((tm, W), lambda i: (i, 0)),
        out_shape=jax.ShapeDtypeStruct((T, W), out_dtype),
        compiler_params=_params(1),
        name="rmsnorm",
    )(x, g)


def _proj_a_body(h_ref, w_ref, gq_ref, gkv_ref, wuq_ref, wukv_ref, ct_ref, st_ref,
                 qa_ref, ka_ref, vat_ref):
    pa = jnp.dot(h_ref[...], w_ref[...], preferred_element_type=F32)
    cq = pa[:, :Q_LORA_PAD]
    ckv = pa[:, Q_LORA_PAD:Q_LORA_PAD + KV_LORA]
    kr = pa[:, Q_LORA_PAD + KV_LORA:]
    cqn = (_rms(cq, Q_LORA) * gq_ref[...]).astype(BF16)
    ckvn = (_rms(ckv) * gkv_ref[...]).astype(BF16)
    ct = ct_ref[...]
    st = st_ref[...]
    kpe = _rope(kr, ct, st).astype(BF16)
    qa = jnp.dot(cqn, wuq_ref[...], preferred_element_type=F32)
    kva = jnp.dot(ckvn, wukv_ref[...], preferred_element_type=F32)
    for hh in range(A_HEADS):
        c0 = hh * A_QK
        qa_ref[:, c0:c0 + A_NOPE] = (qa[:, c0:c0 + A_NOPE] * SCALE_A).astype(BF16)
        qa_ref[:, c0 + A_NOPE:c0 + A_QK] = (
            _rope(qa[:, c0 + A_NOPE:c0 + A_QK], ct, st) * SCALE_A).astype(BF16)
        ka_ref[:, c0:c0 + A_NOPE] = kva[:, hh * A_NOPE:(hh + 1) * A_NOPE].astype(BF16)
        ka_ref[:, c0 + A_NOPE:c0 + A_QK] = kpe
    vat_ref[0] = kva[:, A_HEADS * A_NOPE:].T.astype(BF16)


def _proj_b_body(h_ref, w_ref, gq_ref, gk_ref, ct_ref, st_ref, q_ref, k_ref, vt_ref):
    pb = jnp.dot(h_ref[...], w_ref[...], preferred_element_type=F32)
    ct = ct_ref[...]
    st = st_ref[...]
    for i in range(B_HEADS):
        t = _rms(pb[:, i * HEAD_DIM:(i + 1) * HEAD_DIM]) * gq_ref[...]
        q_ref[:, i * HEAD_DIM:(i + 1) * HEAD_DIM] = (_rope(t, ct, st) * SCALE_H).astype(BF16)
    k0 = B_HEADS * HEAD_DIM
    for i in range(B_KV_HEADS):
        t = _rms(pb[:, k0 + i * HEAD_DIM:k0 + (i + 1) * HEAD_DIM]) * gk_ref[...]
        k_ref[:, i * HEAD_DIM:(i + 1) * HEAD_DIM] = _rope(t, ct, st).astype(BF16)
    vt_ref[0] = pb[:, k0 + B_KV_HEADS * HEAD_DIM:].T.astype(BF16)


def _proj_c_body(h_ref, w_ref, ct_ref, st_ref, q_ref, k_ref, vt_ref):
    pc = jnp.dot(h_ref[...], w_ref[...], preferred_element_type=F32)
    ct = ct_ref[...]
    st = st_ref[...]
    W = C_HEADS * HEAD_DIM
    for i in range(C_HEADS):
        sl = slice(i * HEAD_DIM, (i + 1) * HEAD_DIM)
        q_ref[:, sl] = (_rope(pc[:, sl], ct, st) * SCALE_H).astype(BF16)
        k_ref[:, sl] = _rope(pc[:, W + i * HEAD_DIM:W + (i + 1) * HEAD_DIM], ct, st).astype(BF16)
    vt_ref[0] = pc[:, 2 * W:].T.astype(BF16)


def _row_spec(tm, w):
    return pl.BlockSpec((tm, w), lambda i: (i, 0))


def _const_spec(shape):
    return pl.BlockSpec(shape, lambda i: (0,) * len(shape))


def _proj_call(body, name, h, consts, tables, row_widths, vt_width, batch, seq, tm=512):
    T = h.shape[0]
    nseq = seq // tm
    tab_spec = pl.BlockSpec((tm, HEAD_DIM), lambda i: (i % nseq, 0))
    return pl.pallas_call(
        body,
        grid=(T // tm,),
        in_specs=([_row_spec(tm, h.shape[1])] + [_const_spec(c.shape) for c in consts]
                  + [tab_spec] * len(tables)),
        out_specs=([_row_spec(tm, w) for w in row_widths]
                   + [pl.BlockSpec((1, vt_width, tm), lambda i: (i // nseq, 0, i % nseq))]),
        out_shape=([jax.ShapeDtypeStruct((T, w), BF16) for w in row_widths]
                   + [jax.ShapeDtypeStruct((batch, vt_width, seq), BF16)]),
        compiler_params=_params(1),
        name=name,
    )(h, *consts, *tables)


def _attn_body(*refs, dk, dv, tq, tk, nk, nq, reach):
    if reach is None:
        q_ref, k_ref, vt_ref, o_ref, s0_ref, s1_ref, m_ref, p_ref = refs
        bias_ref = None
    else:
        q_ref, k_ref, vt_ref, bias_ref, o_ref, s0_ref, s1_ref, m_ref, p_ref = refs
    s_refs = (s0_ref, s1_ref)
    i = pl.program_id(2)
    n_chunks = nk if reach is None else 2 * reach + 1

    def chunks_of(qi):
        if reach is None:
            return [(slice(c * tk, (c + 1) * tk), None) for c in range(nk)]
        out = []
        for off in range(-reach, reach + 1):
            j = qi + off
            in_range = jnp.logical_and(j >= 0, j < nk)
            start = pl.multiple_of(jnp.clip(j, 0, nk - 1) * tk, tk)
            out.append((pl.ds(start, tk), jnp.where(in_range, off + reach, 2 * reach + 1)))
        return out

    def run(score_slot, value_slot):
        if score_slot is not None:
            q = q_ref[0]
            cur_chunks = chunks_of(i)
            m_new = jnp.full((1, tq), MASK_VALUE, F32)
        if value_slot is not None:
            m = m_ref[value_slot]
            l = jnp.zeros((1, tq), F32)
        for c in range(n_chunks):
            rows = slice(c * tk, (c + 1) * tk)
            if score_slot is not None:
                kv, bias_idx = cur_chunks[c]
                s = lax.dot_general(k_ref[0, kv, :], q, (((1,), (1,)), ((), ())),
                                    preferred_element_type=F32)
                if bias_idx is not None:
                    s = s + bias_ref[bias_idx]
                s_refs[score_slot][rows, :] = s
                m_new = jnp.maximum(m_new, jnp.max(s, axis=0, keepdims=True))
            if value_slot is not None:
                p = jnp.exp2(s_refs[value_slot][rows, :] - m)
                l = l + jnp.sum(p, axis=0, keepdims=True)
                p_ref[rows, :] = p.astype(BF16)
        if score_slot is not None:
            m_ref[score_slot] = m_new
        if value_slot is not None:
            if reach is None:
                vt = vt_ref[0]
            else:
                vt = jnp.concatenate([vt_ref[0, :, kv] for kv, _ in chunks_of(i - 1)], axis=1)
            acc = jnp.dot(vt, p_ref[...], preferred_element_type=F32)
            o_ref[0] = (acc / l).T.astype(o_ref.dtype)

    pl.when(i == 0)(lambda: run(0, None))
    for parity in (0, 1):
        mid = jnp.logical_and(jnp.logical_and(i > 0, i < nq), i % 2 == parity)
        pl.when(mid)(functools.partial(run, parity, 1 - parity))
    pl.when(i == nq)(lambda: run(None, (nq - 1) % 2))


def _attention(q, k, vt, *, heads, q_per_kv, dk, dv, tq, tk, bias=None, reach=None, name):
    B, S, _ = q.shape
    nk, nq = S // tk, S // tq
    in_specs = [pl.BlockSpec((1, tq, dk), lambda b, h, i: (b, jnp.minimum(i, nq - 1), h)),
                pl.BlockSpec((1, S, dk), lambda b, h, i: (b, 0, h // q_per_kv)),
                pl.BlockSpec((1, dv, S), lambda b, h, i: (b, h // q_per_kv, 0))]
    args = [q, k, vt]
    if bias is not None:
        in_specs.append(pl.BlockSpec(bias.shape, lambda b, h, i: (0, 0, 0)))
        args.append(bias)
    n_keys = (nk if reach is None else 2 * reach + 1) * tk
    return pl.pallas_call(
        functools.partial(_attn_body, dk=dk, dv=dv, tq=tq, tk=tk, nk=nk, nq=nq, reach=reach),
        grid=(B, heads, nq + 1),
        in_specs=in_specs,
        out_specs=pl.BlockSpec((1, tq, dv), lambda b, h, i: (b, jnp.maximum(i - 1, 0), h)),
        out_shape=jax.ShapeDtypeStruct((B, S, heads * dv), BF16),
        scratch_shapes=[pltpu.VMEM((n_keys, tq), F32), pltpu.VMEM((n_keys, tq), F32),
                        pltpu.VMEM((2, 1, tq), F32), pltpu.VMEM((n_keys, tq), BF16)],
        compiler_params=_params(3),
        name=name,
    )(*args)


def _dilated_bias():
    off = (np.arange(-C_REACH, C_REACH + 1)[:, None, None] * ATT_BLK
           + np.arange(ATT_BLK)[None, :, None] - np.arange(ATT_BLK)[None, None, :])
    count = np.zeros(off.shape, np.float64)
    for window, dil in C_BRANCHES:
        count += (off % dil == 0) & (np.abs(off) <= (window // (2 * dil)) * dil)
    bias = np.where(count > 0, np.log2(np.maximum(count, 1.0)), MASK_VALUE)
    bias = np.concatenate([bias, np.full((1, ATT_BLK, ATT_BLK), MASK_VALUE)], axis=0)
    return jnp.asarray(bias, F32)


def _out_body(oa_ref, ob_ref, oc_ref, g_ref, w_ref, x_ref, g2_ref, xo_ref, h2_ref):
    mixed = jnp.concatenate([_rms(r[...].astype(F32)) for r in (oa_ref, ob_ref, oc_ref)], axis=1)
    mixed = (mixed * g_ref[...]).astype(BF16)
    y = x_ref[...] + jnp.dot(mixed, w_ref[...], preferred_element_type=F32)
    xo_ref[...] = y
    h2_ref[...] = (_rms(y) * g2_ref[...]).astype(BF16)


def _out_proj(oa, ob, oc, g_out, w_out, x, ln2_g, tm=512):
    T = x.shape[0]
    return pl.pallas_call(
        _out_body,
        grid=(T // tm,),
        in_specs=[_row_spec(tm, oa.shape[1]), _row_spec(tm, ob.shape[1]), _row_spec(tm, oc.shape[1]),
                  _const_spec(g_out.shape), _const_spec(w_out.shape), _row_spec(tm, D_MODEL),
                  _const_spec(ln2_g.shape)],
        out_specs=[_row_spec(tm, D_MODEL), _row_spec(tm, D_MODEL)],
        out_shape=[jax.ShapeDtypeStruct((T, D_MODEL), F32), jax.ShapeDtypeStruct((T, D_MODEL), BF16)],
        compiler_params=_params(1),
        name="out_proj",
    )(oa, ob, oc, g_out, w_out, x, ln2_g)


def _ff1_body(a_ref, w_ref, o_ref):
    acc = jnp.dot(a_ref[...], w_ref[...], preferred_element_type=F32)
    o_ref[...] = jnp.square(jnp.maximum(acc, 0.0)).astype(o_ref.dtype)


def _ff1(a, w, tm=1024, tn=1024):
    M, K = a.shape
    N = w.shape[1]
    return pl.pallas_call(
        _ff1_body,
        grid=(M // tm, N // tn),
        in_specs=[pl.BlockSpec((tm, K), lambda i, j: (i, 0)),
                  pl.BlockSpec((K, tn), lambda i, j: (0, j))],
        out_specs=pl.BlockSpec((tm, tn), lambda i, j: (i, j)),
        out_shape=jax.ShapeDtypeStruct((M, N), BF16),
        compiler_params=_params(2),
        name="ff1",
    )(a, w)


def _ff2_body(a_ref, w_ref, x_ref, o_ref, acc_ref):
    k = pl.program_id(2)

    @pl.when(k == 0)
    def _():
        acc_ref[...] = x_ref[...]

    acc_ref[...] += jnp.dot(a_ref[...], w_ref[...], preferred_element_type=F32)

    @pl.when(k == pl.num_programs(2) - 1)
    def _():
        o_ref[...] = acc_ref[...]


def _ff2(a, w, x, tm=1024, tn=1024, tk=2048):
    M, K = a.shape
    N = w.shape[1]
    return pl.pallas_call(
        _ff2_body,
        grid=(M // tm, N // tn, K // tk),
        in_specs=[pl.BlockSpec((tm, tk), lambda i, j, k: (i, k)),
                  pl.BlockSpec((tk, tn), lambda i, j, k: (k, j)),
                  pl.BlockSpec((tm, tn), lambda i, j, k: (i, j))],
        out_specs=pl.BlockSpec((tm, tn), lambda i, j, k: (i, j)),
        out_shape=jax.ShapeDtypeStruct((M, N), F32),
        scratch_shapes=[pltpu.VMEM((tm, tn), F32)],
        compiler_params=_params(3),
        name="ff2",
    )(a, w, x)


def _rope_tables(S):
    def angles(pos, dim):
        inv = jnp.power(ROPE_THETA, -jnp.arange(0, dim, 2, dtype=F32) / dim)
        ang = pos.astype(F32)[:, None] * inv[None, :]
        return jnp.cos(ang), jnp.sin(ang)

    t = jnp.arange(S)
    cos_p, sin_p = angles(t, A_ROPE)
    cos_r, sin_r = angles(t // GRID_W, HEAD_DIM // 2)
    cos_c, sin_c = angles(t % GRID_W, HEAD_DIM // 2)
    cos_f, sin_f = angles(t, HEAD_DIM)
    ta = (jnp.concatenate([cos_p, cos_p, cos_p, cos_p], 1), jnp.concatenate([-sin_p, sin_p, sin_p, sin_p], 1))
    tb = (jnp.concatenate([cos_r, cos_c, cos_r, cos_c], 1), jnp.concatenate([-sin_r, -sin_c, sin_r, sin_c], 1))
    tc = (jnp.concatenate([cos_f, cos_f], 1), jnp.concatenate([-sin_f, sin_f], 1))
    return ta, tb, tc


def _axial_perm():
    q = HEAD_DIM // 4
    return np.concatenate([np.arange(0, q), np.arange(2 * q, 3 * q), np.arange(q, 2 * q), np.arange(3 * q, 4 * q)])


def _spread_rope(w):
    z = jnp.zeros(w.shape[:-1] + (A_ROPE // 2,), w.dtype)
    return jnp.concatenate([w[..., :A_ROPE // 2], z, w[..., A_ROPE // 2:], z], axis=-1)


def _prep_weights(w_in, g_q_a, w_uq, g_kv_a, w_ukv, g_qn_b, g_kn_b):
    L = w_in.shape[0]
    sizes = (Q_LORA, KV_LORA, A_ROPE, B_HEADS * HEAD_DIM, B_KV_HEADS * HEAD_DIM, B_KV_HEADS * HEAD_DIM,
             C_HEADS * HEAD_DIM, C_HEADS * HEAD_DIM, C_HEADS * HEAD_DIM)
    offs = np.concatenate([[0], np.cumsum(sizes)])
    cols = [w_in[:, :, offs[i]:offs[i + 1]] for i in range(len(sizes))]
    perm = _axial_perm()

    def perm_heads(w, n):
        w = w.reshape(L, D_MODEL, n, HEAD_DIM)[..., perm]
        return w.reshape(L, D_MODEL, n * HEAD_DIM)

    w_a = jnp.concatenate([cols[0], jnp.zeros((L, D_MODEL, Q_LORA_PAD - Q_LORA), w_in.dtype),
                           cols[1], _spread_rope(cols[2])], axis=-1).astype(BF16)
    w_b = jnp.concatenate([perm_heads(cols[3], B_HEADS), perm_heads(cols[4], B_KV_HEADS), cols[5]],
                          axis=-1).astype(BF16)
    w_c = jnp.concatenate(cols[6:9], axis=-1).astype(BF16)

    uq = w_uq.reshape(L, Q_LORA, A_HEADS, A_NOPE + A_ROPE)
    uq = jnp.concatenate([uq[..., :A_NOPE], _spread_rope(uq[..., A_NOPE:])], axis=-1)
    uq = uq.reshape(L, Q_LORA, A_HEADS * A_QK)
    uq = jnp.concatenate([uq, jnp.zeros((L, Q_LORA_PAD - Q_LORA, A_HEADS * A_QK), uq.dtype)], axis=1).astype(BF16)
    ukv = w_ukv.reshape(L, KV_LORA, A_HEADS, A_NOPE + A_V)
    ukv = jnp.concatenate([ukv[..., :A_NOPE].reshape(L, KV_LORA, A_HEADS * A_NOPE),
                           ukv[..., A_NOPE:].reshape(L, KV_LORA, A_HEADS * A_V)], axis=-1).astype(BF16)
    gq = jnp.concatenate([g_q_a, jnp.zeros((L, Q_LORA_PAD - Q_LORA), g_q_a.dtype)], axis=-1)
    return dict(w_a=w_a, w_b=w_b, w_c=w_c, uq=uq, ukv=ukv, gq=gq[:, None, :], gkv=g_kv_a[:, None, :],
                gqb=g_qn_b[:, perm][:, None, :], gkb=g_kn_b[:, perm][:, None, :])


def kernel(x, ln1_g, w_in, g_q_a, w_uq, g_kv_a, w_ukv, g_qn_b, g_kn_b, g_out, w_out, ln2_g, w_ff1, w_ff2, ln_f_g):
    B, S, D = x.shape
    T = B * S
    depth = w_in.shape[0]
    p = _prep_weights(w_in, g_q_a, w_uq, g_kv_a, w_ukv, g_qn_b, g_kn_b)
    w_out_b = w_out.astype(BF16)
    w_ff1_b = w_ff1.astype(BF16)
    w_ff2_b = w_ff2.astype(BF16)
    ta, tb, tc = _rope_tables(S)
    bias_c = _dilated_bias()

    xt = x.reshape(T, D)
    for l in range(depth):
        h = _rmsnorm(xt, ln1_g[l][None, :], BF16)
        qa, ka, vat = _proj_call(_proj_a_body, "proj_a", h,
                                 [p["w_a"][l], p["gq"][l], p["gkv"][l], p["uq"][l], p["ukv"][l]], ta,
                                 (A_HEADS * A_QK, A_HEADS * A_QK), A_HEADS * A_V, B, S)
        qb, kb, vbt = _proj_call(_proj_b_body, "proj_b", h, [p["w_b"][l], p["gqb"][l], p["gkb"][l]], tb,
                                 (B_HEADS * HEAD_DIM, B_KV_HEADS * HEAD_DIM), B_KV_HEADS * HEAD_DIM, B, S)
        qc, kc, vct = _proj_call(_proj_c_body, "proj_c", h, [p["w_c"][l]], tc,
                                 (C_HEADS * HEAD_DIM,) * 2, C_HEADS * HEAD_DIM, B, S)

        def b3(a):
            return a.reshape(B, S, a.shape[-1])

        o_a = _attention(b3(qa), b3(ka), vat, heads=A_HEADS, q_per_kv=1, dk=A_QK, dv=A_V,
                         tq=ATT_BLK, tk=ATT_BLK, name="attn_a")
        o_b = _attention(b3(qb), b3(kb), vbt, heads=B_HEADS, q_per_kv=B_GROUP, dk=HEAD_DIM, dv=HEAD_DIM,
                         tq=ATT_BLK, tk=ATT_BLK, name="attn_b")
        o_c = _attention(b3(qc), b3(kc), vct, heads=C_HEADS, q_per_kv=1, dk=HEAD_DIM, dv=HEAD_DIM,
                         tq=ATT_BLK, tk=ATT_BLK, bias=bias_c, reach=C_REACH, name="attn_c")

        xt, h2 = _out_proj(o_a.reshape(T, -1), o_b.reshape(T, -1), o_c.reshape(T, -1),
                           g_out[l][None, :], w_out_b[l], xt, ln2_g[l][None, :])
        u = _ff1(h2, w_ff1_b[l])
        xt = _ff2(u, w_ff2_b[l], xt)

    return _rmsnorm(xt, ln_f_g[None, :], F32).reshape(B, S, D)
```

```python
import functools
import math

import jax
import jax.numpy as jnp
import numpy as np
from jax import lax
from jax.experimental import pallas as pl
from jax.experimental.pallas import tpu as pltpu

F32 = jnp.float32
BF16 = jnp.bfloat16

D_MODEL = 2048
HEAD_DIM = 128
A_HEADS, A_NOPE, A_ROPE, A_V = 4, 128, 64, 128
Q_LORA, KV_LORA = 448, 512
Q_LORA_PAD = 512
B_HEADS, B_KV_HEADS = 6, 2
B_GROUP = B_HEADS // B_KV_HEADS
GRID_W = 64
C_HEADS = 6
C_BRANCHES = ((128, 1), (512, 4), (2048, 16))
D_FF = 4 * D_MODEL
ROPE_THETA = 10000.0
EPS = 1e-6
MASK_VALUE = -1e30
LOG2E = math.log2(math.e)
SCALE_A = LOG2E / math.sqrt(A_NOPE + A_ROPE)
SCALE_H = LOG2E / math.sqrt(HEAD_DIM)

A_QK = 256
ATT_BLK = 512
C_REACH = 2
VMEM_LIMIT = 56 * 2**20


def _params(n_axes):
    return pltpu.CompilerParams(dimension_semantics=("arbitrary",) * n_axes,
                                vmem_limit_bytes=VMEM_LIMIT)


def _rms(t, n=None):
    n = t.shape[-1] if n is None else n
    return t * lax.rsqrt(jnp.sum(t * t, axis=-1, keepdims=True) * (1.0 / n) + EPS)


def _rope(t, ct, st):
    return t * ct + pltpu.roll(t, 64, 1) * st


def _rmsnorm_body(x_ref, g_ref, o_ref):
    o_ref[...] = (_rms(x_ref[...]) * g_ref[...]).astype(o_ref.dtype)


def _rmsnorm(x, g, out_dtype, tm=512):
    T, W = x.shape
    return pl.pallas_call(
        _rmsnorm_body,
        grid=(T // tm,),
        in_specs=[pl.BlockSpec((tm, W), lambda i: (i, 0)),
                  pl.BlockSpec((1, W), lambda i: (0, 0))],
        out_specs=pl.BlockSpec((tm, W), lambda i: (i, 0)),
        out_shape=jax.ShapeDtypeStruct((T, W), out_dtype),
        compiler_params=_params(1),
        name="rmsnorm",
    )(x, g)


def _proj_a_body(h_ref, w_ref, gq_ref, gkv_ref, wuq_ref, wukv_ref, ct_ref, st_ref,
                 qa_ref, ka_ref, vat_ref):
    h = h_ref[...]
    cq = jnp.dot(h, w_ref[:, :Q_LORA_PAD], preferred_element_type=F32)
    cqn = (_rms(cq, Q_LORA) * gq_ref[...]).astype(BF16)
    ckr = jnp.dot(h, w_ref[:, Q_LORA_PAD:], preferred_element_type=F32)
    qa = jnp.dot(cqn, wuq_ref[...], preferred_element_type=F32)
    ckvn = (_rms(ckr[:, :KV_LORA]) * gkv_ref[...]).astype(BF16)
    ct = ct_ref[...]
    st = st_ref[...]
    kpe = _rope(ckr[:, KV_LORA:], ct, st).astype(BF16)
    kva = jnp.dot(ckvn, wukv_ref[...], preferred_element_type=F32)
    for hh in range(A_HEADS):
        c0 = hh * A_QK
        qa_ref[:, c0:c0 + A_NOPE] = (qa[:, c0:c0 + A_NOPE] * SCALE_A).astype(BF16)
        qa_ref[:, c0 + A_NOPE:c0 + A_QK] = (
            _rope(qa[:, c0 + A_NOPE:c0 + A_QK], ct, st) * SCALE_A).astype(BF16)
        ka_ref[:, c0:c0 + A_NOPE] = kva[:, hh * A_NOPE:(hh + 1) * A_NOPE].astype(BF16)
        ka_ref[:, c0 + A_NOPE:c0 + A_QK] = kpe
    vat_ref[0] = kva[:, A_HEADS * A_NOPE:].T.astype(BF16)


def _proj_b_body(h_ref, w_ref, gq_ref, gk_ref, ct_ref, st_ref, q_ref, k_ref, vt_ref):
    h = h_ref[...]
    ct = ct_ref[...]
    st = st_ref[...]

    def pair(col):
        return jnp.dot(h, w_ref[:, col:col + 2 * HEAD_DIM], preferred_element_type=F32)

    for i in range(0, B_HEADS, 2):
        pq = pair(i * HEAD_DIM)
        for j in range(2):
            t = _rms(pq[:, j * HEAD_DIM:(j + 1) * HEAD_DIM]) * gq_ref[...]
            q_ref[:, (i + j) * HEAD_DIM:(i + j + 1) * HEAD_DIM] = (_rope(t, ct, st) * SCALE_H).astype(BF16)
    k0 = B_HEADS * HEAD_DIM
    pk = pair(k0)
    for i in range(B_KV_HEADS):
        t = _rms(pk[:, i * HEAD_DIM:(i + 1) * HEAD_DIM]) * gk_ref[...]
        k_ref[:, i * HEAD_DIM:(i + 1) * HEAD_DIM] = _rope(t, ct, st).astype(BF16)
    vt_ref[0] = pair(k0 + B_KV_HEADS * HEAD_DIM).T.astype(BF16)


def _proj_c_body(h_ref, w_ref, ct_ref, st_ref, q_ref, k_ref, vt_ref):
    pc = jnp.dot(h_ref[...], w_ref[...], preferred_element_type=F32)
    ct = ct_ref[...]
    st = st_ref[...]
    W = C_HEADS * HEAD_DIM
    for i in range(C_HEADS):
        sl = slice(i * HEAD_DIM, (i + 1) * HEAD_DIM)
        q_ref[:, sl] = (_rope(pc[:, sl], ct, st) * SCALE_H).astype(BF16)
        k_ref[:, sl] = _rope(pc[:, W + i * HEAD_DIM:W + (i + 1) * HEAD_DIM], ct, st).astype(BF16)
    vt_ref[0] = pc[:, 2 * W:].T.astype(BF16)


def _row_spec(tm, w):
    return pl.BlockSpec((tm, w), lambda i: (i, 0))


def _const_spec(shape):
    return pl.BlockSpec(shape, lambda i: (0,) * len(shape))


def _proj_call(body, name, h, consts, tables, row_widths, vt_width, batch, seq, tm=512):
    T = h.shape[0]
    nseq = seq // tm
    tab_spec = pl.BlockSpec((tm, HEAD_DIM), lambda i: (i % nseq, 0))
    return pl.pallas_call(
        body,
        grid=(T // tm,),
        in_specs=([_row_spec(tm, h.shape[1])] + [_const_spec(c.shape) for c in consts]
                  + [tab_spec] * len(tables)),
        out_specs=([_row_spec(tm, w) for w in row_widths]
                   + [pl.BlockSpec((1, vt_width, tm), lambda i: (i // nseq, 0, i % nseq))]),
        out_shape=([jax.ShapeDtypeStruct((T, w), BF16) for w in row_widths]
                   + [jax.ShapeDtypeStruct((batch, vt_width, seq), BF16)]),
        compiler_params=_params(1),
        name=name,
    )(h, *consts, *tables)


def _attn_body(*refs, dk, dv, tq, tk, nk, nq, reach):
    if reach is None:
        q_ref, k_ref, vt_ref, o_ref, s0_ref, s1_ref, m_ref, p_ref = refs
        bias_ref = None
    else:
        q_ref, k_ref, vt_ref, bias_ref, o_ref, s0_ref, s1_ref, m_ref, p_ref = refs
    s_refs = (s0_ref, s1_ref)
    n_chunks = nk if reach is None else 2 * reach + 1

    def block(idx, size):
        if isinstance(idx, int):
            return slice(idx * size, (idx + 1) * size)
        return pl.ds(pl.multiple_of(idx * size, size), size)

    def chunks_of(t):
        if reach is None:
            return [(block(c, tk), None) for c in range(nk)]
        out = []
        for off in range(-reach, reach + 1):
            j = t + off
            if isinstance(t, int):
                in_range = 0 <= j < nk
                out.append((block(min(max(j, 0), nk - 1), tk), off + reach if in_range else 2 * reach + 1))
            else:
                in_range = jnp.logical_and(j >= 0, j < nk)
                out.append((block(jnp.clip(j, 0, nk - 1), tk), jnp.where(in_range, off + reach, 2 * reach + 1)))
        return out

    def stage(t, parity, do_scores, do_values):
        score_slot, value_slot = parity, 1 - parity
        if do_scores:
            q = q_ref[0, block(t, tq), :]
            cur_chunks = chunks_of(t)
            m_new = jnp.full((1, tq), MASK_VALUE, F32)
        if do_values:
            m = m_ref[value_slot]
            l = jnp.zeros((1, tq), F32)
        for c in range(n_chunks):
            rows = block(c, tk)
            if do_scores:
                kv, bias_idx = cur_chunks[c]
                s = lax.dot_general(k_ref[0, kv, :], q, (((1,), (1,)), ((), ())),
                                    preferred_element_type=F32)
                if bias_idx is not None:
                    s = s + bias_ref[bias_idx]
                s_refs[score_slot][rows, :] = s
                m_new = jnp.maximum(m_new, jnp.max(s, axis=0, keepdims=True))
            if do_values:
                p = jnp.exp2(s_refs[value_slot][rows, :] - m)
                l = l + jnp.sum(p, axis=0, keepdims=True)
                p_ref[rows, :] = p.astype(BF16)
        if do_scores:
            m_ref[score_slot] = m_new
        if do_values:
            if reach is None:
                vt = vt_ref[0]
            else:
                vt = jnp.concatenate([vt_ref[0, :, kv] for kv, _ in chunks_of(t - 1)], axis=1)
            acc = jnp.dot(vt, p_ref[...], preferred_element_type=F32)
            o_ref[0, block(t - 1, tq), :] = (acc / l).T.astype(o_ref.dtype)

    stage(0, 0, True, False)

    def stage_pair(j, carry):
        stage(2 * j + 1, 1, True, True)
        stage(2 * j + 2, 0, True, True)
        return carry

    lax.fori_loop(0, (nq - 2) // 2, stage_pair, 0)
    stage(nq - 1, 1, True, True)
    stage(nq, 0, False, True)


def _attention(q, k, vt, *, heads, q_per_kv, dk, dv, tq, tk, bias=None, reach=None, name):
    B, S, _ = q.shape
    nk, nq = S // tk, S // tq
    assert nq >= 2 and nq % 2 == 0, "stages are paired by parity"
    in_specs = [pl.BlockSpec((1, S, dk), lambda b, h: (b, 0, h)),
                pl.BlockSpec((1, S, dk), lambda b, h: (b, 0, h // q_per_kv)),
                pl.BlockSpec((1, dv, S), lambda b, h: (b, h // q_per_kv, 0))]
    args = [q, k, vt]
    if bias is not None:
        in_specs.append(pl.BlockSpec(bias.shape, lambda b, h: (0, 0, 0)))
        args.append(bias)
    n_keys = (nk if reach is None else 2 * reach + 1) * tk
    return pl.pallas_call(
        functools.partial(_attn_body, dk=dk, dv=dv, tq=tq, tk=tk, nk=nk, nq=nq, reach=reach),
        grid=(B, heads),
        in_specs=in_specs,
        out_specs=pl.BlockSpec((1, S, dv), lambda b, h: (b, 0, h)),
        out_shape=jax.ShapeDtypeStruct((B, S, heads * dv), BF16),
        scratch_shapes=[pltpu.VMEM((n_keys, tq), F32), pltpu.VMEM((n_keys, tq), F32),
                        pltpu.VMEM((2, 1, tq), F32), pltpu.VMEM((n_keys, tq), BF16)],
        compiler_params=_params(2),
        name=name,
    )(*args)


def _dilated_bias():
    off = (np.arange(-C_REACH, C_REACH + 1)[:, None, None] * ATT_BLK
           + np.arange(ATT_BLK)[None, :, None] - np.arange(ATT_BLK)[None, None, :])
    count = np.zeros(off.shape, np.float64)
    for window, dil in C_BRANCHES:
        count += (off % dil == 0) & (np.abs(off) <= (window // (2 * dil)) * dil)
    bias = np.where(count > 0, np.log2(np.maximum(count, 1.0)), MASK_VALUE)
    bias = np.concatenate([bias, np.full((1, ATT_BLK, ATT_BLK), MASK_VALUE)], axis=0)
    return jnp.asarray(bias, F32)


def _out_body(oa_ref, ob_ref, oc_ref, g_ref, w_ref, x_ref, g2_ref, xo_ref, h2_ref):
    mixed = jnp.concatenate([_rms(r[...].astype(F32)) for r in (oa_ref, ob_ref, oc_ref)], axis=1)
    mixed = (mixed * g_ref[...]).astype(BF16)
    y = x_ref[...] + jnp.dot(mixed, w_ref[...], preferred_element_type=F32)
    xo_ref[...] = y
    h2_ref[...] = (_rms(y) * g2_ref[...]).astype(BF16)


def _out_proj(oa, ob, oc, g_out, w_out, x, ln2_g, tm=512):
    T = x.shape[0]
    return pl.pallas_call(
        _out_body,
        grid=(T // tm,),
        in_specs=[_row_spec(tm, oa.shape[1]), _row_spec(tm, ob.shape[1]), _row_spec(tm, oc.shape[1]),
                  _const_spec(g_out.shape), _const_spec(w_out.shape), _row_spec(tm, D_MODEL),
                  _const_spec(ln2_g.shape)],
        out_specs=[_row_spec(tm, D_MODEL), _row_spec(tm, D_MODEL)],
        out_shape=[jax.ShapeDtypeStruct((T, D_MODEL), F32), jax.ShapeDtypeStruct((T, D_MODEL), BF16)],
        compiler_params=_params(1),
        name="out_proj",
    )(oa, ob, oc, g_out, w_out, x, ln2_g)


def _ff1_body(a_ref, w_ref, o_ref):
    acc = jnp.dot(a_ref[...], w_ref[...], preferred_element_type=F32)
    o_ref[...] = jnp.square(jnp.maximum(acc, 0.0)).astype(o_ref.dtype)


def _ff1(a, w, tm=1024, tn=1024):
    M, K = a.shape
    N = w.shape[1]
    return pl.pallas_call(
        _ff1_body,
        grid=(M // tm, N // tn),
        in_specs=[pl.BlockSpec((tm, K), lambda i, j: (i, 0)),
                  pl.BlockSpec((K, tn), lambda i, j: (0, j))],
        out_specs=pl.BlockSpec((tm, tn), lambda i, j: (i, j)),
        out_shape=jax.ShapeDtypeStruct((M, N), BF16),
        compiler_params=_params(2),
        name="ff1",
    )(a, w)


def _ff2_body(a_ref, w_ref, x_ref, o_ref, acc_ref):
    k = pl.program_id(2)
    last = pl.num_programs(2) - 1

    def partial_product():
        return jnp.dot(a_ref[...], w_ref[...], preferred_element_type=F32)

    @pl.when(k == 0)
    def _():
        acc_ref[...] = x_ref[...] + partial_product()

    @pl.when(jnp.logical_and(k > 0, k < last))
    def _():
        acc_ref[...] += partial_product()

    @pl.when(k == last)
    def _():
        o_ref[...] = acc_ref[...] + partial_product()


def _ff2(a, w, x, tm=1024, tn=1024, tk=2048):
    M, K = a.shape
    N = w.shape[1]
    return pl.pallas_call(
        _ff2_body,
        grid=(M // tm, N // tn, K // tk),
        in_specs=[pl.BlockSpec((tm, tk), lambda i, j, k: (i, k)),
                  pl.BlockSpec((tk, tn), lambda i, j, k: (k, j)),
                  pl.BlockSpec((tm, tn), lambda i, j, k: (i, j))],
        out_specs=pl.BlockSpec((tm, tn), lambda i, j, k: (i, j)),
        out_shape=jax.ShapeDtypeStruct((M, N), F32),
        scratch_shapes=[pltpu.VMEM((tm, tn), F32)],
        compiler_params=_params(3),
        name="ff2",
    )(a, w, x)


def _rope_tables(S):
    def angles(pos, dim):
        inv = jnp.power(ROPE_THETA, -jnp.arange(0, dim, 2, dtype=F32) / dim)
        ang = pos.astype(F32)[:, None] * inv[None, :]
        return jnp.cos(ang), jnp.sin(ang)

    t = jnp.arange(S)
    cos_p, sin_p = angles(t, A_ROPE)
    cos_r, sin_r = angles(t // GRID_W, HEAD_DIM // 2)
    cos_c, sin_c = angles(t % GRID_W, HEAD_DIM // 2)
    cos_f, sin_f = angles(t, HEAD_DIM)
    ta = (jnp.concatenate([cos_p, cos_p, cos_p, cos_p], 1), jnp.concatenate([-sin_p, sin_p, sin_p, sin_p], 1))
    tb = (jnp.concatenate([cos_r, cos_c, cos_r, cos_c], 1), jnp.concatenate([-sin_r, -sin_c, sin_r, sin_c], 1))
    tc = (jnp.concatenate([cos_f, cos_f], 1), jnp.concatenate([-sin_f, sin_f], 1))
    return ta, tb, tc


def _axial_perm():
    q = HEAD_DIM // 4
    return np.concatenate([np.arange(0, q), np.arange(2 * q, 3 * q), np.arange(q, 2 * q), np.arange(3 * q, 4 * q)])


def _spread_rope(w):
    z = jnp.zeros(w.shape[:-1] + (A_ROPE // 2,), w.dtype)
    return jnp.concatenate([w[..., :A_ROPE // 2], z, w[..., A_ROPE // 2:], z], axis=-1)


def _prep_weights(w_in, g_q_a, w_uq, g_kv_a, w_ukv, g_qn_b, g_kn_b):
    L = w_in.shape[0]
    sizes = (Q_LORA, KV_LORA, A_ROPE, B_HEADS * HEAD_DIM, B_KV_HEADS * HEAD_DIM, B_KV_HEADS * HEAD_DIM,
             C_HEADS * HEAD_DIM, C_HEADS * HEAD_DIM, C_HEADS * HEAD_DIM)
    offs = np.concatenate([[0], np.cumsum(sizes)])
    cols = [w_in[:, :, offs[i]:offs[i + 1]] for i in range(len(sizes))]
    perm = _axial_perm()

    def perm_heads(w, n):
        w = w.reshape(L, D_MODEL, n, HEAD_DIM)[..., perm]
        return w.reshape(L, D_MODEL, n * HEAD_DIM)

    w_a = jnp.concatenate([cols[0], jnp.zeros((L, D_MODEL, Q_LORA_PAD - Q_LORA), w_in.dtype),
                           cols[1], _spread_rope(cols[2])], axis=-1).astype(BF16)
    w_b = jnp.concatenate([perm_heads(cols[3], B_HEADS), perm_heads(cols[4], B_KV_HEADS), cols[5]],
                          axis=-1).astype(BF16)
    w_c = jnp.concatenate(cols[6:9], axis=-1).astype(BF16)

    uq = w_uq.reshape(L, Q_LORA, A_HEADS, A_NOPE + A_ROPE)
    uq = jnp.concatenate([uq[..., :A_NOPE], _spread_rope(uq[..., A_NOPE:])], axis=-1)
    uq = uq.reshape(L, Q_LORA, A_HEADS * A_QK)
    uq = jnp.concatenate([uq, jnp.zeros((L, Q_LORA_PAD - Q_LORA, A_HEADS * A_QK), uq.dtype)], axis=1).astype(BF16)
    ukv = w_ukv.reshape(L, KV_LORA, A_HEADS, A_NOPE + A_V)
    ukv = jnp.concatenate([ukv[..., :A_NOPE].reshape(L, KV_LORA, A_HEADS * A_NOPE),
                           ukv[..., A_NOPE:].reshape(L, KV_LORA, A_HEADS * A_V)], axis=-1).astype(BF16)
    gq = jnp.concatenate([g_q_a, jnp.zeros((L, Q_LORA_PAD - Q_LORA), g_q_a.dtype)], axis=-1)
    return dict(w_a=w_a, w_b=w_b, w_c=w_c, uq=uq, ukv=ukv, gq=gq[:, None, :], gkv=g_kv_a[:, None, :],
                gqb=g_qn_b[:, perm][:, None, :], gkb=g_kn_b[:, perm][:, None, :])


def kernel(x, ln1_g, w_in, g_q_a, w_uq, g_kv_a, w_ukv, g_qn_b, g_kn_b, g_out, w_out, ln2_g, w_ff1, w_ff2, ln_f_g):
    B, S, D = x.shape
    T = B * S
    depth = w_in.shape[0]
    p = _prep_weights(w_in, g_q_a, w_uq, g_kv_a, w_ukv, g_qn_b, g_kn_b)
    w_out_b = w_out.astype(BF16)
    w_ff1_b = w_ff1.astype(BF16)
    w_ff2_b = w_ff2.astype(BF16)
    ta, tb, tc = _rope_tables(S)
    bias_c = _dilated_bias()

    xt = x.reshape(T, D)
    for l in range(depth):
        h = _rmsnorm(xt, ln1_g[l][None, :], BF16)
        qa, ka, vat = _proj_call(_proj_a_body, "proj_a", h,
                                 [p["w_a"][l], p["gq"][l], p["gkv"][l], p["uq"][l], p["ukv"][l]], ta,
                                 (A_HEADS * A_QK, A_HEADS * A_QK), A_HEADS * A_V, B, S)
        qb, kb, vbt = _proj_call(_proj_b_body, "proj_b", h, [p["w_b"][l], p["gqb"][l], p["gkb"][l]], tb,
                                 (B_HEADS * HEAD_DIM, B_KV_HEADS * HEAD_DIM), B_KV_HEADS * HEAD_DIM, B, S)
        qc, kc, vct = _proj_call(_proj_c_body, "proj_c", h, [p["w_c"][l]], tc,
                                 (C_HEADS * HEAD_DIM,) * 2, C_HEADS * HEAD_DIM, B, S)

        def b3(a):
            return a.reshape(B, S, a.shape[-1])

        o_a = _attention(b3(qa), b3(ka), vat, heads=A_HEADS, q_per_kv=1, dk=A_QK, dv=A_V,
                         tq=ATT_BLK, tk=ATT_BLK, name="attn_a")
        o_b = _attention(b3(qb), b3(kb), vbt, heads=B_HEADS, q_per_kv=B_GROUP, dk=HEAD_DIM, dv=HEAD_DIM,
                         tq=ATT_BLK, tk=ATT_BLK, name="attn_b")
        o_c = _attention(b3(qc), b3(kc), vct, heads=C_HEADS, q_per_kv=1, dk=HEAD_DIM, dv=HEAD_DIM,
                         tq=ATT_BLK, tk=ATT_BLK, bias=bias_c, reach=C_REACH, name="attn_c")

        xt, h2 = _out_proj(o_a.reshape(T, -1), o_b.reshape(T, -1), o_c.reshape(T, -1),
                           g_out[l][None, :], w_out_b[l], xt, ln2_g[l][None, :])
        u = _ff1(h2, w_ff1_b[l])
        xt = _ff2(u, w_ff2_b[l], xt)

    return _rmsnorm(xt, ln_f_g[None, :], F32).reshape(B, S, D)
```

```python
import functools
import math

import jax
import jax.numpy as jnp
import numpy as np
from jax import lax
from jax.experimental import pallas as pl
from jax.experimental.pallas import tpu as pltpu

F32 = jnp.float32
BF16 = jnp.bfloat16

D_MODEL = 2048
HEAD_DIM = 128
A_HEADS, A_NOPE, A_ROPE, A_V = 4, 128, 64, 128
Q_LORA, KV_LORA = 448, 512
Q_LORA_PAD = 512
B_HEADS, B_KV_HEADS = 6, 2
B_GROUP = B_HEADS // B_KV_HEADS
GRID_W = 64
C_HEADS = 6
C_BRANCHES = ((128, 1), (512, 4), (2048, 16))
D_FF = 4 * D_MODEL
ROPE_THETA = 10000.0
EPS = 1e-6
MASK_VALUE = -1e30
LOG2E = math.log2(math.e)
SCALE_A = LOG2E / math.sqrt(A_NOPE + A_ROPE)
SCALE_H = LOG2E / math.sqrt(HEAD_DIM)

A_QK = 256
ATT_BLK = 512
C_REACH = 2
VMEM_LIMIT = 56 * 2**20


def _params(n_axes, flags=None):
    return pltpu.CompilerParams(dimension_semantics=("arbitrary",) * n_axes,
                                vmem_limit_bytes=VMEM_LIMIT, flags=flags)


def _rms(t, n=None):
    n = t.shape[-1] if n is None else n
    return t * lax.rsqrt(jnp.sum(t * t, axis=-1, keepdims=True) * (1.0 / n) + EPS)


def _rope(t, ct, st):
    return t * ct + pltpu.roll(t, 64, 1) * st


def _rmsnorm_body(x_ref, g_ref, o_ref):
    o_ref[...] = (_rms(x_ref[...]) * g_ref[...]).astype(o_ref.dtype)


def _rmsnorm(x, g, out_dtype, tm=512):
    T, W = x.shape
    return pl.pallas_call(
        _rmsnorm_body,
        grid=(T // tm,),
        in_specs=[pl.BlockSpec((tm, W), lambda i: (i, 0)),
                  pl.BlockSpec((1, W), lambda i: (0, 0))],
        out_specs=pl.BlockSpec((tm, W), lambda i: (i, 0)),
        out_shape=jax.ShapeDtypeStruct((T, W), out_dtype),
        compiler_params=_params(1),
        name="rmsnorm",
    )(x, g)


def _proj_a_body(h_ref, w_ref, gq_ref, gkv_ref, wuq_ref, wukv_ref, ct_ref, st_ref,
                 qa_ref, ka_ref, vat_ref):
    h = h_ref[...]
    cq = jnp.dot(h, w_ref[:, :Q_LORA_PAD], preferred_element_type=F32)
    cqn = (_rms(cq, Q_LORA) * gq_ref[...]).astype(BF16)
    ckr = jnp.dot(h, w_ref[:, Q_LORA_PAD:], preferred_element_type=F32)
    qa = jnp.dot(cqn, wuq_ref[...], preferred_element_type=F32)
    ckvn = (_rms(ckr[:, :KV_LORA]) * gkv_ref[...]).astype(BF16)
    ct = ct_ref[...]
    st = st_ref[...]
    kpe = _rope(ckr[:, KV_LORA:], ct, st).astype(BF16)
    kva = jnp.dot(ckvn, wukv_ref[...], preferred_element_type=F32)
    for hh in range(A_HEADS):
        c0 = hh * A_QK
        qa_ref[:, c0:c0 + A_NOPE] = (qa[:, c0:c0 + A_NOPE] * SCALE_A).astype(BF16)
        qa_ref[:, c0 + A_NOPE:c0 + A_QK] = (
            _rope(qa[:, c0 + A_NOPE:c0 + A_QK], ct, st) * SCALE_A).astype(BF16)
        ka_ref[:, c0:c0 + A_NOPE] = kva[:, hh * A_NOPE:(hh + 1) * A_NOPE].astype(BF16)
        ka_ref[:, c0 + A_NOPE:c0 + A_QK] = kpe
    vat_ref[0] = kva[:, A_HEADS * A_NOPE:].T.astype(BF16)


def _proj_b_body(h_ref, w_ref, gq_ref, gk_ref, ct_ref, st_ref, q_ref, k_ref, vt_ref):
    h = h_ref[...]
    ct = ct_ref[...]
    st = st_ref[...]

    def pair(col):
        return jnp.dot(h, w_ref[:, col:col + 2 * HEAD_DIM], preferred_element_type=F32)

    for i in range(0, B_HEADS, 2):
        pq = pair(i * HEAD_DIM)
        for j in range(2):
            t = _rms(pq[:, j * HEAD_DIM:(j + 1) * HEAD_DIM]) * gq_ref[...]
            q_ref[:, (i + j) * HEAD_DIM:(i + j + 1) * HEAD_DIM] = (_rope(t, ct, st) * SCALE_H).astype(BF16)
    k0 = B_HEADS * HEAD_DIM
    pk = pair(k0)
    for i in range(B_KV_HEADS):
        t = _rms(pk[:, i * HEAD_DIM:(i + 1) * HEAD_DIM]) * gk_ref[...]
        k_ref[:, i * HEAD_DIM:(i + 1) * HEAD_DIM] = _rope(t, ct, st).astype(BF16)
    vt_ref[0] = pair(k0 + B_KV_HEADS * HEAD_DIM).T.astype(BF16)


def _proj_c_body(h_ref, w_ref, ct_ref, st_ref, q_ref, k_ref, vt_ref):
    pc = jnp.dot(h_ref[...], w_ref[...], preferred_element_type=F32)
    ct = ct_ref[...]
    st = st_ref[...]
    W = C_HEADS * HEAD_DIM
    for i in range(C_HEADS):
        sl = slice(i * HEAD_DIM, (i + 1) * HEAD_DIM)
        q_ref[:, sl] = (_rope(pc[:, sl], ct, st) * SCALE_H).astype(BF16)
        k_ref[:, sl] = _rope(pc[:, W + i * HEAD_DIM:W + (i + 1) * HEAD_DIM], ct, st).astype(BF16)
    vt_ref[0] = pc[:, 2 * W:].T.astype(BF16)


def _row_spec(tm, w):
    return pl.BlockSpec((tm, w), lambda i: (i, 0))


def _const_spec(shape):
    return pl.BlockSpec(shape, lambda i: (0,) * len(shape), pipeline_mode=pl.Buffered(1))


def _proj_call(body, name, h, consts, tables, row_widths, vt_width, batch, seq, tm=512):
    T = h.shape[0]
    nseq = seq // tm
    tab_spec = pl.BlockSpec((tm, HEAD_DIM), lambda i: (i % nseq, 0))
    return pl.pallas_call(
        body,
        grid=(T // tm,),
        in_specs=([_row_spec(tm, h.shape[1])] + [_const_spec(c.shape) for c in consts]
                  + [tab_spec] * len(tables)),
        out_specs=([_row_spec(tm, w) for w in row_widths]
                   + [pl.BlockSpec((1, vt_width, tm), lambda i: (i // nseq, 0, i % nseq))]),
        out_shape=([jax.ShapeDtypeStruct((T, w), BF16) for w in row_widths]
                   + [jax.ShapeDtypeStruct((batch, vt_width, seq), BF16)]),
        compiler_params=_params(1),
        name=name,
    )(h, *consts, *tables)


def _attn_body(*refs, dk, dv, tq, tk, nk, nq, reach):
    if reach is None:
        q_ref, k_ref, vt_ref, o_ref, s0_ref, s1_ref, m_ref = refs
        bias_ref = None
    else:
        q_ref, k_ref, vt_ref, bias_ref, o_ref, s0_ref, s1_ref, m_ref = refs
    s_refs = (s0_ref, s1_ref)
    n_chunks = nk if reach is None else 2 * reach + 1

    def block(idx, size):
        if isinstance(idx, int):
            return slice(idx * size, (idx + 1) * size)
        return pl.ds(pl.multiple_of(idx * size, size), size)

    def chunks_of(t):
        if reach is None:
            return [(block(c, tk), None) for c in range(nk)]
        out = []
        for off in range(-reach, reach + 1):
            j = t + off
            if isinstance(t, int):
                in_range = 0 <= j < nk
                out.append((block(min(max(j, 0), nk - 1), tk), off + reach if in_range else 2 * reach + 1))
            else:
                in_range = jnp.logical_and(j >= 0, j < nk)
                out.append((block(jnp.clip(j, 0, nk - 1), tk), jnp.where(in_range, off + reach, 2 * reach + 1)))
        return out

    def stage(t, parity, do_scores, do_values):
        score_slot, value_slot = parity, 1 - parity
        if do_scores:
            q = q_ref[0, block(t, tq), :]
            cur_chunks = chunks_of(t)
            m_new = jnp.full((1, tq), MASK_VALUE, F32)
        if do_values:
            m = m_ref[value_slot]
            l = jnp.zeros((1, tq), F32)
            acc = jnp.zeros((dv, tq), F32)
            prev_chunks = chunks_of(t - 1)
        for c in range(n_chunks):
            rows = block(c, tk)
            if do_scores:
                kv, bias_idx = cur_chunks[c]
                s = lax.dot_general(k_ref[0, kv, :], q, (((1,), (1,)), ((), ())),
                                    preferred_element_type=F32)
                if bias_idx is not None:
                    s = s + bias_ref[bias_idx]
                s_refs[score_slot][rows, :] = s
                m_new = jnp.maximum(m_new, jnp.max(s, axis=0, keepdims=True))
            if do_values:
                p = jnp.exp2(s_refs[value_slot][rows, :] - m)
                l = l + jnp.sum(p, axis=0, keepdims=True)
                acc = acc + jnp.dot(vt_ref[0, :, prev_chunks[c][0]], p.astype(BF16),
                                    preferred_element_type=F32)
        if do_scores:
            m_ref[score_slot] = m_new
        if do_values:
            o_ref[0, block(t - 1, tq), :] = (acc / l).T.astype(o_ref.dtype)

    stage(0, 0, True, False)

    def stage_pair(j, carry):
        stage(2 * j + 1, 1, True, True)
        stage(2 * j + 2, 0, True, True)
        return carry

    lax.fori_loop(0, (nq - 2) // 2, stage_pair, 0)
    stage(nq - 1, 1, True, True)
    stage(nq, 0, False, True)


def _attention(q, k, vt, *, heads, q_per_kv, dk, dv, tq, tk, bias=None, reach=None, name):
    B, S, _ = q.shape
    nk, nq = S // tk, S // tq
    assert nq >= 2 and nq % 2 == 0, "stages are paired by parity"
    in_specs = [pl.BlockSpec((1, S, dk), lambda b, h: (b, 0, h)),
                pl.BlockSpec((1, S, dk), lambda b, h: (b, 0, h // q_per_kv)),
                pl.BlockSpec((1, dv, S), lambda b, h: (b, h // q_per_kv, 0))]
    args = [q, k, vt]
    if bias is not None:
        in_specs.append(pl.BlockSpec(bias.shape, lambda b, h: (0, 0, 0)))
        args.append(bias)
    n_keys = (nk if reach is None else 2 * reach + 1) * tk
    return pl.pallas_call(
        functools.partial(_attn_body, dk=dk, dv=dv, tq=tq, tk=tk, nk=nk, nq=nq, reach=reach),
        grid=(B, heads),
        in_specs=in_specs,
        out_specs=pl.BlockSpec((1, S, dv), lambda b, h: (b, 0, h)),
        out_shape=jax.ShapeDtypeStruct((B, S, heads * dv), BF16),
        scratch_shapes=[pltpu.VMEM((n_keys, tq), F32), pltpu.VMEM((n_keys, tq), F32),
                        pltpu.VMEM((2, 1, tq), F32)],
        compiler_params=_params(2),
        name=name,
    )(*args)


def _dilated_bias():
    off = (np.arange(-C_REACH, C_REACH + 1)[:, None, None] * ATT_BLK
           + np.arange(ATT_BLK)[None, :, None] - np.arange(ATT_BLK)[None, None, :])
    count = np.zeros(off.shape, np.float64)
    for window, dil in C_BRANCHES:
        count += (off % dil == 0) & (np.abs(off) <= (window // (2 * dil)) * dil)
    bias = np.where(count > 0, np.log2(np.maximum(count, 1.0)), MASK_VALUE)
    bias = np.concatenate([bias, np.full((1, ATT_BLK, ATT_BLK), MASK_VALUE)], axis=0)
    return jnp.asarray(bias, F32)


def _out_body(oa_ref, ob_ref, oc_ref, g_ref, w_ref, x_ref, g2_ref, xo_ref, h2_ref):
    mixed = jnp.concatenate([_rms(r[...].astype(F32)) for r in (oa_ref, ob_ref, oc_ref)], axis=1)
    mixed = (mixed * g_ref[...]).astype(BF16)
    y = x_ref[...] + jnp.dot(mixed, w_ref[...], preferred_element_type=F32)
    xo_ref[...] = y
    h2_ref[...] = (_rms(y) * g2_ref[...]).astype(BF16)


def _out_proj(oa, ob, oc, g_out, w_out, x, ln2_g, tm=512):
    T = x.shape[0]
    return pl.pallas_call(
        _out_body,
        grid=(T // tm,),
        in_specs=[_row_spec(tm, oa.shape[1]), _row_spec(tm, ob.shape[1]), _row_spec(tm, oc.shape[1]),
                  _const_spec(g_out.shape), _const_spec(w_out.shape), _row_spec(tm, D_MODEL),
                  _const_spec(ln2_g.shape)],
        out_specs=[_row_spec(tm, D_MODEL), _row_spec(tm, D_MODEL)],
        out_shape=[jax.ShapeDtypeStruct((T, D_MODEL), F32), jax.ShapeDtypeStruct((T, D_MODEL), BF16)],
        compiler_params=_params(1),
        name="out_proj",
    )(oa, ob, oc, g_out, w_out, x, ln2_g)


def _ff1_body(a_ref, w_ref, o_ref):
    acc = jnp.dot(a_ref[...], w_ref[...], preferred_element_type=F32)
    o_ref[...] = jnp.square(jnp.maximum(acc, 0.0)).astype(o_ref.dtype)


def _ff1(a, w, tm=1024, tn=1024):
    M, K = a.shape
    N = w.shape[1]
    return pl.pallas_call(
        _ff1_body,
        grid=(M // tm, N // tn),
        in_specs=[pl.BlockSpec((tm, K), lambda i, j: (i, 0)),
                  pl.BlockSpec((K, tn), lambda i, j: (0, j))],
        out_specs=pl.BlockSpec((tm, tn), lambda i, j: (i, j)),
        out_shape=jax.ShapeDtypeStruct((M, N), BF16),
        compiler_params=_params(2),
        name="ff1",
    )(a, w)


def _ff2_body(a_ref, w_ref, x_ref, o_ref, acc_ref):
    k = pl.program_id(2)
    last = pl.num_programs(2) - 1

    def partial_product():
        return jnp.dot(a_ref[...], w_ref[...], preferred_element_type=F32)

    @pl.when(k == 0)
    def _():
        acc_ref[...] = x_ref[...] + partial_product()

    @pl.when(jnp.logical_and(k > 0, k < last))
    def _():
        acc_ref[...] += partial_product()

    @pl.when(k == last)
    def _():
        o_ref[...] = acc_ref[...] + partial_product()


def _ff2(a, w, x, tm=1024, tn=1024, tk=2048):
    M, K = a.shape
    N = w.shape[1]
    return pl.pallas_call(
        _ff2_body,
        grid=(M // tm, N // tn, K // tk),
        in_specs=[pl.BlockSpec((tm, tk), lambda i, j, k: (i, k)),
                  pl.BlockSpec((tk, tn), lambda i, j, k: (k, j)),
                  pl.BlockSpec((tm, tn), lambda i, j, k: (i, j))],
        out_specs=pl.BlockSpec((tm, tn), lambda i, j, k: (i, j)),
        out_shape=jax.ShapeDtypeStruct((M, N), F32),
        scratch_shapes=[pltpu.VMEM((tm, tn), F32)],
        compiler_params=_params(3),
        name="ff2",
    )(a, w, x)


def _rope_tables(S):
    def angles(pos, dim):
        inv = jnp.power(ROPE_THETA, -jnp.arange(0, dim, 2, dtype=F32) / dim)
        ang = pos.astype(F32)[:, None] * inv[None, :]
        return jnp.cos(ang), jnp.sin(ang)

    t = jnp.arange(S)
    cos_p, sin_p = angles(t, A_ROPE)
    cos_r, sin_r = angles(t // GRID_W, HEAD_DIM // 2)
    cos_c, sin_c = angles(t % GRID_W, HEAD_DIM // 2)
    cos_f, sin_f = angles(t, HEAD_DIM)
    ta = (jnp.concatenate([cos_p, cos_p, cos_p, cos_p], 1), jnp.concatenate([-sin_p, sin_p, sin_p, sin_p], 1))
    tb = (jnp.concatenate([cos_r, cos_c, cos_r, cos_c], 1), jnp.concatenate([-sin_r, -sin_c, sin_r, sin_c], 1))
    tc = (jnp.concatenate([cos_f, cos_f], 1), jnp.concatenate([-sin_f, sin_f], 1))
    return ta, tb, tc


def _axial_perm():
    q = HEAD_DIM // 4
    return np.concatenate([np.arange(0, q), np.arange(2 * q, 3 * q), np.arange(q, 2 * q), np.arange(3 * q, 4 * q)])


def _spread_rope(w):
    z = jnp.zeros(w.shape[:-1] + (A_ROPE // 2,), w.dtype)
    return jnp.concatenate([w[..., :A_ROPE // 2], z, w[..., A_ROPE // 2:], z], axis=-1)


def _prep_weights(w_in, g_q_a, w_uq, g_kv_a, w_ukv, g_qn_b, g_kn_b):
    L = w_in.shape[0]
    sizes = (Q_LORA, KV_LORA, A_ROPE, B_HEADS * HEAD_DIM, B_KV_HEADS * HEAD_DIM, B_KV_HEADS * HEAD_DIM,
             C_HEADS * HEAD_DIM, C_HEADS * HEAD_DIM, C_HEADS * HEAD_DIM)
    offs = np.concatenate([[0], np.cumsum(sizes)])
    cols = [w_in[:, :, offs[i]:offs[i + 1]] for i in range(len(sizes))]
    perm = _axial_perm()

    def perm_heads(w, n):
        w = w.reshape(L, D_MODEL, n, HEAD_DIM)[..., perm]
        return w.reshape(L, D_MODEL, n * HEAD_DIM)

    w_a = jnp.concatenate([cols[0], jnp.zeros((L, D_MODEL, Q_LORA_PAD - Q_LORA), w_in.dtype),
                           cols[1], _spread_rope(cols[2])], axis=-1).astype(BF16)
    w_b = jnp.concatenate([perm_heads(cols[3], B_HEADS), perm_heads(cols[4], B_KV_HEADS), cols[5]],
                          axis=-1).astype(BF16)
    w_c = jnp.concatenate(cols[6:9], axis=-1).astype(BF16)

    uq = w_uq.reshape(L, Q_LORA, A_HEADS, A_NOPE + A_ROPE)
    uq = jnp.concatenate([uq[..., :A_NOPE], _spread_rope(uq[..., A_NOPE:])], axis=-1)
    uq = uq.reshape(L, Q_LORA, A_HEADS * A_QK)
    uq = jnp.concatenate([uq, jnp.zeros((L, Q_LORA_PAD - Q_LORA, A_HEADS * A_QK), uq.dtype)], axis=1).astype(BF16)
    ukv = w_ukv.reshape(L, KV_LORA, A_HEADS, A_NOPE + A_V)
    ukv = jnp.concatenate([ukv[..., :A_NOPE].reshape(L, KV_LORA, A_HEADS * A_NOPE),
                           ukv[..., A_NOPE:].reshape(L, KV_LORA, A_HEADS * A_V)], axis=-1).astype(BF16)
    gq = jnp.concatenate([g_q_a, jnp.zeros((L, Q_LORA_PAD - Q_LORA), g_q_a.dtype)], axis=-1)
    return dict(w_a=w_a, w_b=w_b, w_c=w_c, uq=uq, ukv=ukv, gq=gq[:, None, :], gkv=g_kv_a[:, None, :],
                gqb=g_qn_b[:, perm][:, None, :], gkb=g_kn_b[:, perm][:, None, :])


def kernel(x, ln1_g, w_in, g_q_a, w_uq, g_kv_a, w_ukv, g_qn_b, g_kn_b, g_out, w_out, ln2_g, w_ff1, w_ff2, ln_f_g):
    B, S, D = x.shape
    T = B * S
    depth = w_in.shape[0]
    p = _prep_weights(w_in, g_q_a, w_uq, g_kv_a, w_ukv, g_qn_b, g_kn_b)
    w_out_b = w_out.astype(BF16)
    w_ff1_b = w_ff1.astype(BF16)
    w_ff2_b = w_ff2.astype(BF16)
    ta, tb, tc = _rope_tables(S)
    bias_c = _dilated_bias()

    xt = x.reshape(T, D)
    for l in range(depth):
        h = _rmsnorm(xt, ln1_g[l][None, :], BF16)
        qa, ka, vat = _proj_call(_proj_a_body, "proj_a", h,
                                 [p["w_a"][l], p["gq"][l], p["gkv"][l], p["uq"][l], p["ukv"][l]], ta,
                                 (A_HEADS * A_QK, A_HEADS * A_QK), A_HEADS * A_V, B, S)
        qb, kb, vbt = _proj_call(_proj_b_body, "proj_b", h, [p["w_b"][l], p["gqb"][l], p["gkb"][l]], tb,
                                 (B_HEADS * HEAD_DIM, B_KV_HEADS * HEAD_DIM), B_KV_HEADS * HEAD_DIM, B, S)
        qc, kc, vct = _proj_call(_proj_c_body, "proj_c", h, [p["w_c"][l]], tc,
                                 (C_HEADS * HEAD_DIM,) * 2, C_HEADS * HEAD_DIM, B, S)

        def b3(a):
            return a.reshape(B, S, a.shape[-1])

        o_a = _attention(b3(qa), b3(ka), vat, heads=A_HEADS, q_per_kv=1, dk=A_QK, dv=A_V,
                         tq=ATT_BLK, tk=ATT_BLK, name="attn_a")
        o_b = _attention(b3(qb), b3(kb), vbt, heads=B_HEADS, q_per_kv=B_GROUP, dk=HEAD_DIM, dv=HEAD_DIM,
                         tq=ATT_BLK, tk=ATT_BLK, name="attn_b")
        o_c = _attention(b3(qc), b3(kc), vct, heads=C_HEADS, q_per_kv=1, dk=HEAD_DIM, dv=HEAD_DIM,
                         tq=ATT_BLK, tk=ATT_BLK, bias=bias_c, reach=C_REACH, name="attn_c")

        xt, h2 = _out_proj(o_a.reshape(T, -1), o_b.reshape(T, -1), o_c.reshape(T, -1),
                           g_out[l][None, :], w_out_b[l], xt, ln2_g[l][None, :])
        u = _ff1(h2, w_ff1_b[l])
        xt = _ff2(u, w_ff2_b[l], xt)

    return _rmsnorm(xt, ln_f_g[None, :], F32).reshape(B, S, D)
```

```python
import functools
import math

import jax
import jax.numpy as jnp
import numpy as np
from jax import lax
from jax.experimental import pallas as pl
from jax.experimental.pallas import tpu as pltpu

F32 = jnp.float32
BF16 = jnp.bfloat16

D_MODEL = 2048
HEAD_DIM = 128
A_HEADS, A_NOPE, A_ROPE, A_V = 4, 128, 64, 128
Q_LORA, KV_LORA = 448, 512
Q_LORA_PAD = 512
B_HEADS, B_KV_HEADS = 6, 2
B_GROUP = B_HEADS // B_KV_HEADS
GRID_W = 64
C_HEADS = 6
C_BRANCHES = ((128, 1), (512, 4), (2048, 16))
D_FF = 4 * D_MODEL
ROPE_THETA = 10000.0
EPS = 1e-6
MASK_VALUE = -1e30
LOG2E = math.log2(math.e)
SCALE_A = LOG2E / math.sqrt(A_NOPE + A_ROPE)
SCALE_H = LOG2E / math.sqrt(HEAD_DIM)

A_QK = 256
ATT_BLK = 512
C_REACH = 1
VMEM_LIMIT = 56 * 2**20


def _params(n_axes, flags=None):
    return pltpu.CompilerParams(dimension_semantics=("arbitrary",) * n_axes,
                                vmem_limit_bytes=VMEM_LIMIT, flags=flags)


def _rms(t, n=None):
    n = t.shape[-1] if n is None else n
    return t * lax.rsqrt(jnp.sum(t * t, axis=-1, keepdims=True) * (1.0 / n) + EPS)


def _rope(t, ct, st):
    return t * ct + pltpu.roll(t, 64, 1) * st


def _rmsnorm_body(x_ref, g_ref, o_ref):
    o_ref[...] = (_rms(x_ref[...]) * g_ref[...]).astype(o_ref.dtype)


def _layer_spec(stacked, layer):
    index = (layer,) + (0,) * (stacked.ndim - 1)
    return pl.BlockSpec((None,) + stacked.shape[1:], lambda *_: index, pipeline_mode=pl.Buffered(1))


def _rmsnorm(x, g, layer, out_dtype, tm=512):
    T, W = x.shape
    return pl.pallas_call(
        _rmsnorm_body,
        grid=(T // tm,),
        in_specs=[pl.BlockSpec((tm, W), lambda i: (i, 0)), _layer_spec(g, layer)],
        out_specs=pl.BlockSpec((tm, W), lambda i: (i, 0)),
        out_shape=jax.ShapeDtypeStruct((T, W), out_dtype),
        compiler_params=_params(1),
        name="rmsnorm",
    )(x, g)


def _proj_a_body(h_ref, w_ref, gq_ref, gkv_ref, wuq_ref, wukv_ref, ct_ref, st_ref,
                 qa_ref, ka_ref, vat_ref):
    h = h_ref[...]
    cq = jnp.dot(h, w_ref[:, :Q_LORA_PAD], preferred_element_type=F32)
    cqn = (_rms(cq, Q_LORA) * gq_ref[...]).astype(BF16)
    ckr = jnp.dot(h, w_ref[:, Q_LORA_PAD:], preferred_element_type=F32)
    qa = jnp.dot(cqn, wuq_ref[...], preferred_element_type=F32)
    ckvn = (_rms(ckr[:, :KV_LORA]) * gkv_ref[...]).astype(BF16)
    ct = ct_ref[...]
    st = st_ref[...]
    kpe = _rope(ckr[:, KV_LORA:], ct, st).astype(BF16)
    kva = jnp.dot(ckvn, wukv_ref[...], preferred_element_type=F32)
    for hh in range(A_HEADS):
        c0 = hh * A_QK
        qa_ref[:, c0:c0 + A_NOPE] = (qa[:, c0:c0 + A_NOPE] * SCALE_A).astype(BF16)
        qa_ref[:, c0 + A_NOPE:c0 + A_QK] = (
            _rope(qa[:, c0 + A_NOPE:c0 + A_QK], ct, st) * SCALE_A).astype(BF16)
        ka_ref[:, c0:c0 + A_NOPE] = kva[:, hh * A_NOPE:(hh + 1) * A_NOPE].astype(BF16)
        ka_ref[:, c0 + A_NOPE:c0 + A_QK] = kpe
    vat_ref[0] = kva[:, A_HEADS * A_NOPE:].T.astype(BF16)


def _proj_b_body(h_ref, w_ref, gq_ref, gk_ref, ct_ref, st_ref, q_ref, k_ref, vt_ref):
    h = h_ref[...]
    ct = ct_ref[...]
    st = st_ref[...]

    def pair(col):
        return jnp.dot(h, w_ref[:, col:col + 2 * HEAD_DIM], preferred_element_type=F32)

    for i in range(0, B_HEADS, 2):
        pq = pair(i * HEAD_DIM)
        for j in range(2):
            t = _rms(pq[:, j * HEAD_DIM:(j + 1) * HEAD_DIM]) * gq_ref[...]
            q_ref[:, (i + j) * HEAD_DIM:(i + j + 1) * HEAD_DIM] = (_rope(t, ct, st) * SCALE_H).astype(BF16)
    k0 = B_HEADS * HEAD_DIM
    pk = pair(k0)
    for i in range(B_KV_HEADS):
        t = _rms(pk[:, i * HEAD_DIM:(i + 1) * HEAD_DIM]) * gk_ref[...]
        k_ref[:, i * HEAD_DIM:(i + 1) * HEAD_DIM] = _rope(t, ct, st).astype(BF16)
    vt_ref[0] = pair(k0 + B_KV_HEADS * HEAD_DIM).T.astype(BF16)


def _proj_c_body(h_ref, w_ref, ct_ref, st_ref, q_ref, k_ref, v_ref, vt_ref):
    pc = jnp.dot(h_ref[...], w_ref[...], preferred_element_type=F32)
    ct = ct_ref[...]
    st = st_ref[...]
    W = C_HEADS * HEAD_DIM
    for i in range(C_HEADS):
        sl = slice(i * HEAD_DIM, (i + 1) * HEAD_DIM)
        q_ref[:, sl] = (_rope(pc[:, sl], ct, st) * SCALE_H).astype(BF16)
        k_ref[:, sl] = _rope(pc[:, W + i * HEAD_DIM:W + (i + 1) * HEAD_DIM], ct, st).astype(BF16)
    v_ref[...] = pc[:, 2 * W:].astype(BF16)
    vt_ref[0] = pc[:, 2 * W:].T.astype(BF16)


def _row_spec(tm, w):
    return pl.BlockSpec((tm, w), lambda i: (i, 0))


def _proj_call(body, name, h, consts, layer, tables, row_widths, vt_width, batch, seq, tm=512):
    T = h.shape[0]
    nseq = seq // tm
    tab_spec = pl.BlockSpec((tm, HEAD_DIM), lambda i: (i % nseq, 0))
    return pl.pallas_call(
        body,
        grid=(T // tm,),
        in_specs=([_row_spec(tm, h.shape[1])] + [_layer_spec(c, layer) for c in consts]
                  + [tab_spec] * len(tables)),
        out_specs=([_row_spec(tm, w) for w in row_widths]
                   + [pl.BlockSpec((1, vt_width, tm), lambda i: (i // nseq, 0, i % nseq))]),
        out_shape=([jax.ShapeDtypeStruct((T, w), BF16) for w in row_widths]
                   + [jax.ShapeDtypeStruct((batch, vt_width, seq), BF16)]),
        compiler_params=_params(1),
        name=name,
    )(h, *consts, *tables)


def _attn_body(*refs, dk, dv, tq, tk, nk, nq, reach, merge):
    refs = list(refs)
    q_ref, k_ref, vt_ref = refs[:3]
    del refs[:3]
    bias_ref = refs.pop(0) if reach is not None else None
    o2_ref, lse2_ref = (refs.pop(0), refs.pop(0)) if merge else (None, None)
    o_ref, s0_ref, s1_ref, m_ref = refs
    s_refs = (s0_ref, s1_ref)
    n_chunks = nk if reach is None else 2 * reach + 1

    def block(idx, size):
        if isinstance(idx, int):
            return slice(idx * size, (idx + 1) * size)
        return pl.ds(pl.multiple_of(idx * size, size), size)

    def chunks_of(t):
        if reach is None:
            return [(block(c, tk), None) for c in range(nk)]
        out = []
        for off in range(-reach, reach + 1):
            j = t + off
            if isinstance(t, int):
                in_range = 0 <= j < nk
                out.append((block(min(max(j, 0), nk - 1), tk), off + reach if in_range else 2 * reach + 1))
            else:
                in_range = jnp.logical_and(j >= 0, j < nk)
                out.append((block(jnp.clip(j, 0, nk - 1), tk), jnp.where(in_range, off + reach, 2 * reach + 1)))
        return out

    def stage(t, parity, do_scores, do_values):
        score_slot, value_slot = parity, 1 - parity
        if do_scores:
            q = q_ref[0, block(t, tq), :]
            cur_chunks = chunks_of(t)
            m_new = jnp.full((1, tq), MASK_VALUE, F32)
        if do_values:
            m = m_ref[value_slot]
            l = jnp.zeros((1, tq), F32)
            acc = jnp.zeros((dv, tq), F32)
            prev_chunks = chunks_of(t - 1)
        for c in range(n_chunks):
            rows = block(c, tk)
            if do_scores:
                kv, bias_idx = cur_chunks[c]
                s = lax.dot_general(k_ref[0, kv, :], q, (((1,), (1,)), ((), ())),
                                    preferred_element_type=F32)
                if bias_idx is not None:
                    s = s + bias_ref[bias_idx]
                s_refs[score_slot][rows, :] = s
                m_new = jnp.maximum(m_new, jnp.max(s, axis=0, keepdims=True))
            if do_values:
                p = jnp.exp2(s_refs[value_slot][rows, :] - m)
                l = l + jnp.sum(p, axis=0, keepdims=True)
                acc = acc + jnp.dot(vt_ref[0, :, prev_chunks[c][0]], p.astype(BF16),
                                    preferred_element_type=F32)
        if do_scores:
            m_ref[score_slot] = m_new
        if do_values:
            rows_q = block(t - 1, tq)
            if merge:
                o2 = o2_ref[0, rows_q, :].astype(F32).T
                lse2 = lse2_ref[0, rows_q, :].T[0:1, :]
                top = jnp.maximum(m, lse2)
                w1, w2 = jnp.exp2(m - top), jnp.exp2(lse2 - top)
                acc, l = acc * w1 + o2 * w2, l * w1 + w2
            o_ref[0, rows_q, :] = (acc / l).T.astype(o_ref.dtype)

    stage(0, 0, True, False)

    def stage_pair(j, carry):
        stage(2 * j + 1, 1, True, True)
        stage(2 * j + 2, 0, True, True)
        return carry

    lax.fori_loop(0, (nq - 2) // 2, stage_pair, 0)
    stage(nq - 1, 1, True, True)
    stage(nq, 0, False, True)


def _attention(q, k, vt, *, heads, q_per_kv, dk, dv, tq, tk, bias=None, reach=None, merge=None, name):
    B, S, _ = q.shape
    nk, nq = S // tk, S // tq
    assert nq >= 2 and nq % 2 == 0, "stages are paired by parity"
    in_specs = [pl.BlockSpec((1, S, dk), lambda b, h: (b, 0, h)),
                pl.BlockSpec((1, S, dk), lambda b, h: (b, 0, h // q_per_kv)),
                pl.BlockSpec((1, dv, S), lambda b, h: (b, h // q_per_kv, 0))]
    args = [q, k, vt]
    if bias is not None:
        in_specs.append(pl.BlockSpec(bias.shape, lambda b, h: (0, 0, 0), pipeline_mode=pl.Buffered(1)))
        args.append(bias)
    if merge is not None:
        in_specs += [pl.BlockSpec((1, S, dv), lambda b, h: (b, 0, h))] * 2
        args += list(merge)
    n_keys = (nk if reach is None else 2 * reach + 1) * tk
    return pl.pallas_call(
        functools.partial(_attn_body, dk=dk, dv=dv, tq=tq, tk=tk, nk=nk, nq=nq, reach=reach,
                          merge=merge is not None),
        grid=(B, heads),
        in_specs=in_specs,
        out_specs=pl.BlockSpec((1, S, dv), lambda b, h: (b, 0, h)),
        out_shape=jax.ShapeDtypeStruct((B, S, heads * dv), BF16),
        scratch_shapes=[pltpu.VMEM((n_keys, tq), F32), pltpu.VMEM((n_keys, tq), F32),
                        pltpu.VMEM((2, 1, tq), F32)],
        compiler_params=_params(2),
        name=name,
    )(*args)


def _dilated_bias(branches, reach):
    off = (np.arange(-reach, reach + 1)[:, None, None] * ATT_BLK
           + np.arange(ATT_BLK)[None, :, None] - np.arange(ATT_BLK)[None, None, :])
    count = np.zeros(off.shape, np.float64)
    for window, dil in branches:
        count += (off % dil == 0) & (np.abs(off) <= (window // (2 * dil)) * dil)
    bias = np.where(count > 0, np.log2(np.maximum(count, 1.0)), MASK_VALUE)
    bias = np.concatenate([bias, np.full((1, ATT_BLK, ATT_BLK), MASK_VALUE)], axis=0)
    return jnp.asarray(bias, F32)


def _far_branch_body(q_ref, k_ref, v_ref, band_ref, o_ref, lse_ref, *, n_sub):
    band = band_ref[...]
    for i in range(n_sub):
        cols = slice(i * HEAD_DIM, (i + 1) * HEAD_DIM)
        s = lax.dot_general(q_ref[0, :, cols], k_ref[0, :, cols], (((1,), (1,)), ((), ())),
                            preferred_element_type=F32) + band
        m = jnp.max(s, axis=1, keepdims=True)
        p = jnp.exp2(s - m)
        l = jnp.sum(p, axis=1, keepdims=True)
        o = jnp.dot(p.astype(BF16), v_ref[0, :, cols], preferred_element_type=F32)
        o_ref[0, :, cols] = (o / l).astype(o_ref.dtype)
        lse_ref[0, :, cols] = jnp.broadcast_to(m + jnp.log2(l), (s.shape[0], HEAD_DIM))


def _far_branch(q, k, v, *, window, dil, groups=4):
    B, S, W = q.shape
    L = S // dil
    half = window // (2 * dil)
    idx = np.arange(L)
    band = jnp.asarray(np.where(np.abs(idx[:, None] - idx[None, :]) <= half, 0.0, MASK_VALUE), F32)
    wblk = dil // groups * W
    view = lambda a: a.reshape(B, L, dil * W)
    spec = pl.BlockSpec((1, L, wblk), lambda b, g: (b, 0, g))
    o, lse = pl.pallas_call(
        functools.partial(_far_branch_body, n_sub=wblk // HEAD_DIM),
        grid=(B, groups),
        in_specs=[spec, spec, spec, pl.BlockSpec((L, L), lambda b, g: (0, 0), pipeline_mode=pl.Buffered(1))],
        out_specs=[spec, spec],
        out_shape=[jax.ShapeDtypeStruct((B, L, dil * W), BF16), jax.ShapeDtypeStruct((B, L, dil * W), F32)],
        compiler_params=_params(2),
        name="attn_c_far",
    )(view(q), view(k), view(v), band)
    return o.reshape(B, S, W), lse.reshape(B, S, W)


def _out_body(oa_ref, ob_ref, oc_ref, g_ref, w_ref, x_ref, g2_ref, xo_ref, h2_ref, wb_ref):
    @pl.when(pl.program_id(0) == 0)
    def _():
        wb_ref[...] = w_ref[...].astype(BF16)

    mixed = jnp.concatenate([_rms(r[...].astype(F32)) for r in (oa_ref, ob_ref, oc_ref)], axis=1)
    mixed = (mixed * g_ref[...]).astype(BF16)
    y = x_ref[...] + jnp.dot(mixed, wb_ref[...], preferred_element_type=F32)
    xo_ref[...] = y
    h2_ref[...] = (_rms(y) * g2_ref[...]).astype(BF16)


def _out_proj(oa, ob, oc, g_out, w_out, x, ln2_g, layer, tm=512):
    T = x.shape[0]
    return pl.pallas_call(
        _out_body,
        grid=(T // tm,),
        in_specs=[_row_spec(tm, oa.shape[1]), _row_spec(tm, ob.shape[1]), _row_spec(tm, oc.shape[1]),
                  _layer_spec(g_out, layer), _layer_spec(w_out, layer), _row_spec(tm, D_MODEL),
                  _layer_spec(ln2_g, layer)],
        out_specs=[_row_spec(tm, D_MODEL), _row_spec(tm, D_MODEL)],
        out_shape=[jax.ShapeDtypeStruct((T, D_MODEL), F32), jax.ShapeDtypeStruct((T, D_MODEL), BF16)],
        scratch_shapes=[pltpu.VMEM(w_out.shape[1:], BF16)],
        compiler_params=_params(1),
        name="out_proj",
    )(oa, ob, oc, g_out, w_out, x, ln2_g)


def _ff1_body(a_ref, w_ref, o_ref, wb_ref):
    @pl.when(pl.program_id(1) == 0)
    def _():
        wb_ref[...] = w_ref[...].astype(BF16)

    acc = jnp.dot(a_ref[...], wb_ref[...], preferred_element_type=F32)
    o_ref[...] = jnp.square(jnp.maximum(acc, 0.0)).astype(o_ref.dtype)


def _ff1(a, w, layer, tm=1024, tn=1024):
    M, K = a.shape
    N = w.shape[2]
    return pl.pallas_call(
        _ff1_body,
        grid=(N // tn, M // tm),
        in_specs=[pl.BlockSpec((tm, K), lambda j, i: (i, 0)),
                  pl.BlockSpec((None, K, tn), lambda j, i: (layer, 0, j))],
        out_specs=pl.BlockSpec((tm, tn), lambda j, i: (i, j)),
        out_shape=jax.ShapeDtypeStruct((M, N), BF16),
        scratch_shapes=[pltpu.VMEM((K, tn), BF16)],
        compiler_params=_params(2),
        name="ff1",
    )(a, w)


def _ff2_body(a_ref, w_ref, x_ref, o_ref, acc_ref):
    k = pl.program_id(2)
    last = pl.num_programs(2) - 1

    def partial_product():
        return jnp.dot(a_ref[...], w_ref[...], preferred_element_type=F32)

    @pl.when(k == 0)
    def _():
        acc_ref[...] = x_ref[...] + partial_product()

    @pl.when(jnp.logical_and(k > 0, k < last))
    def _():
        acc_ref[...] += partial_product()

    @pl.when(k == last)
    def _():
        o_ref[...] = acc_ref[...] + partial_product()


def _ff2(a, w, x, layer, tm=1024, tn=1024, tk=2048):
    M, K = a.shape
    N = w.shape[2]
    return pl.pallas_call(
        _ff2_body,
        grid=(M // tm, N // tn, K // tk),
        in_specs=[pl.BlockSpec((tm, tk), lambda i, j, k: (i, k)),
                  pl.BlockSpec((None, tk, tn), lambda i, j, k: (layer, k, j)),
                  pl.BlockSpec((tm, tn), lambda i, j, k: (i, j))],
        out_specs=pl.BlockSpec((tm, tn), lambda i, j, k: (i, j)),
        out_shape=jax.ShapeDtypeStruct((M, N), F32),
        scratch_shapes=[pltpu.VMEM((tm, tn), F32)],
        compiler_params=_params(3),
        name="ff2",
    )(a, w, x)


def _rope_tables(S):
    def angles(pos, dim):
        inv = jnp.power(ROPE_THETA, -jnp.arange(0, dim, 2, dtype=F32) / dim)
        ang = pos.astype(F32)[:, None] * inv[None, :]
        return jnp.cos(ang), jnp.sin(ang)

    t = jnp.arange(S)
    cos_p, sin_p = angles(t, A_ROPE)
    cos_r, sin_r = angles(t // GRID_W, HEAD_DIM // 2)
    cos_c, sin_c = angles(t % GRID_W, HEAD_DIM // 2)
    cos_f, sin_f = angles(t, HEAD_DIM)
    ta = (jnp.concatenate([cos_p, cos_p, cos_p, cos_p], 1), jnp.concatenate([-sin_p, sin_p, sin_p, sin_p], 1))
    tb = (jnp.concatenate([cos_r, cos_c, cos_r, cos_c], 1), jnp.concatenate([-sin_r, -sin_c, sin_r, sin_c], 1))
    tc = (jnp.concatenate([cos_f, cos_f], 1), jnp.concatenate([-sin_f, sin_f], 1))
    return ta, tb, tc


def _axial_swap(a, lead):
    q = HEAD_DIM // 4
    return a.reshape(lead + (-1, 2, 2, q)).swapaxes(-3, -2).reshape(lead + (-1,))


def _spread_rope(w):
    z = jnp.zeros(w.shape[:-1] + (A_ROPE // 2,), w.dtype)
    return jnp.concatenate([w[..., :A_ROPE // 2], z, w[..., A_ROPE // 2:], z], axis=-1)


def _prep_weights(w_in, g_q_a, w_uq, g_kv_a, w_ukv, g_qn_b, g_kn_b):
    L = w_in.shape[0]
    sizes = (Q_LORA, KV_LORA, A_ROPE, B_HEADS * HEAD_DIM, B_KV_HEADS * HEAD_DIM, B_KV_HEADS * HEAD_DIM,
             C_HEADS * HEAD_DIM, C_HEADS * HEAD_DIM, C_HEADS * HEAD_DIM)
    offs = np.concatenate([[0], np.cumsum(sizes)])
    w_in = w_in.astype(BF16)
    cols = [w_in[:, :, offs[i]:offs[i + 1]] for i in range(len(sizes))]
    lead = (L, D_MODEL)

    w_a = jnp.concatenate([cols[0], jnp.zeros((L, D_MODEL, Q_LORA_PAD - Q_LORA), BF16),
                           cols[1], _spread_rope(cols[2])], axis=-1)
    w_b = jnp.concatenate([_axial_swap(cols[3], lead), _axial_swap(cols[4], lead), cols[5]], axis=-1)
    w_c = w_in[:, :, offs[6]:]

    uq = w_uq.astype(BF16).reshape(L, Q_LORA, A_HEADS, A_NOPE + A_ROPE)
    uq = jnp.concatenate([uq[..., :A_NOPE], _spread_rope(uq[..., A_NOPE:])], axis=-1)
    uq = uq.reshape(L, Q_LORA, A_HEADS * A_QK)
    uq = jnp.concatenate([uq, jnp.zeros((L, Q_LORA_PAD - Q_LORA, A_HEADS * A_QK), BF16)], axis=1)
    ukv = w_ukv.astype(BF16).reshape(L, KV_LORA, A_HEADS, A_NOPE + A_V)
    ukv = jnp.concatenate([ukv[..., :A_NOPE].reshape(L, KV_LORA, A_HEADS * A_NOPE),
                           ukv[..., A_NOPE:].reshape(L, KV_LORA, A_HEADS * A_V)], axis=-1)
    gq = jnp.concatenate([g_q_a, jnp.zeros((L, Q_LORA_PAD - Q_LORA), g_q_a.dtype)], axis=-1)
    return dict(w_a=w_a, w_b=w_b, w_c=w_c, uq=uq, ukv=ukv, gq=gq[:, None, :], gkv=g_kv_a[:, None, :],
                gqb=_axial_swap(g_qn_b, (L,))[:, None, :], gkb=_axial_swap(g_kn_b, (L,))[:, None, :])


def kernel(x, ln1_g, w_in, g_q_a, w_uq, g_kv_a, w_ukv, g_qn_b, g_kn_b, g_out, w_out, ln2_g, w_ff1, w_ff2, ln_f_g):
    B, S, D = x.shape
    T = B * S
    depth = w_in.shape[0]
    p = _prep_weights(w_in, g_q_a, w_uq, g_kv_a, w_ukv, g_qn_b, g_kn_b)
    w_ff2_b = w_ff2.astype(BF16)
    ln1, ln2, g_o = ln1_g[:, None, :], ln2_g[:, None, :], g_out[:, None, :]
    ta, tb, tc = _rope_tables(S)
    bias_c = _dilated_bias(C_BRANCHES[:-1], C_REACH)

    xt = x.reshape(T, D)
    for l in range(depth):
        h = _rmsnorm(xt, ln1, l, BF16)
        qa, ka, vat = _proj_call(_proj_a_body, "proj_a", h,
                                 [p["w_a"], p["gq"], p["gkv"], p["uq"], p["ukv"]], l, ta,
                                 (A_HEADS * A_QK, A_HEADS * A_QK), A_HEADS * A_V, B, S)
        qb, kb, vbt = _proj_call(_proj_b_body, "proj_b", h, [p["w_b"], p["gqb"], p["gkb"]], l, tb,
                                 (B_HEADS * HEAD_DIM, B_KV_HEADS * HEAD_DIM), B_KV_HEADS * HEAD_DIM, B, S)
        qc, kc, vc, vct = _proj_call(_proj_c_body, "proj_c", h, [p["w_c"]], l, tc,
                                     (C_HEADS * HEAD_DIM,) * 3, C_HEADS * HEAD_DIM, B, S)

        def b3(a):
            return a.reshape(B, S, a.shape[-1])

        o_a = _attention(b3(qa), b3(ka), vat, heads=A_HEADS, q_per_kv=1, dk=A_QK, dv=A_V,
                         tq=ATT_BLK, tk=ATT_BLK, name="attn_a")
        o_b = _attention(b3(qb), b3(kb), vbt, heads=B_HEADS, q_per_kv=B_GROUP, dk=HEAD_DIM, dv=HEAD_DIM,
                         tq=ATT_BLK, tk=ATT_BLK, name="attn_b")
        far = _far_branch(b3(qc), b3(kc), b3(vc), window=C_BRANCHES[-1][0], dil=C_BRANCHES[-1][1])
        o_c = _attention(b3(qc), b3(kc), vct, heads=C_HEADS, q_per_kv=1, dk=HEAD_DIM, dv=HEAD_DIM,
                         tq=ATT_BLK, tk=ATT_BLK, bias=bias_c, reach=C_REACH, merge=far, name="attn_c")

        xt, h2 = _out_proj(o_a.reshape(T, -1), o_b.reshape(T, -1), o_c.reshape(T, -1),
                           g_o, w_out, xt, ln2, l)
        u = _ff1(h2, w_ff1, l)
        xt = _ff2(u, w_ff2_b, xt, l)

    return _rmsnorm(xt, ln_f_g[None, None, :], 0, F32).reshape(B, S, D)
```

```python
import functools
import math

import jax
import jax.numpy as jnp
import numpy as np
from jax import lax
from jax.experimental import pallas as pl
from jax.experimental.pallas import tpu as pltpu

F32 = jnp.float32
BF16 = jnp.bfloat16

D_MODEL = 2048
HEAD_DIM = 128
A_HEADS, A_NOPE, A_ROPE, A_V = 4, 128, 64, 128
Q_LORA, KV_LORA = 448, 512
Q_LORA_PAD = 512
B_HEADS, B_KV_HEADS = 6, 2
B_GROUP = B_HEADS // B_KV_HEADS
GRID_W = 64
C_HEADS = 6
C_BRANCHES = ((128, 1), (512, 4), (2048, 16))
D_FF = 4 * D_MODEL
ROPE_THETA = 10000.0
EPS = 1e-6
MASK_VALUE = -1e30
LOG2E = math.log2(math.e)
SCALE_A = LOG2E / math.sqrt(A_NOPE + A_ROPE)
SCALE_H = LOG2E / math.sqrt(HEAD_DIM)

A_QK = 256
ATT_BLK = 512
C_REACH = 2
VMEM_LIMIT = 56 * 2**20


def _params(n_axes, flags=None):
    return pltpu.CompilerParams(dimension_semantics=("arbitrary",) * n_axes,
                                vmem_limit_bytes=VMEM_LIMIT, flags=flags)


def _rms(t, n=None):
    n = t.shape[-1] if n is None else n
    return t * lax.rsqrt(jnp.sum(t * t, axis=-1, keepdims=True) * (1.0 / n) + EPS)


def _rope(t, ct, st):
    return t * ct + pltpu.roll(t, 64, 1) * st


def _rmsnorm_body(x_ref, g_ref, o_ref):
    o_ref[...] = (_rms(x_ref[...]) * g_ref[...]).astype(o_ref.dtype)


def _layer_spec(stacked, layer):
    index = (layer,) + (0,) * (stacked.ndim - 1)
    return pl.BlockSpec((None,) + stacked.shape[1:], lambda *_: index, pipeline_mode=pl.Buffered(1))


def _rmsnorm(x, g, layer, out_dtype, tm=512):
    T, W = x.shape
    return pl.pallas_call(
        _rmsnorm_body,
        grid=(T // tm,),
        in_specs=[pl.BlockSpec((tm, W), lambda i: (i, 0)), _layer_spec(g, layer)],
        out_specs=pl.BlockSpec((tm, W), lambda i: (i, 0)),
        out_shape=jax.ShapeDtypeStruct((T, W), out_dtype),
        compiler_params=_params(1),
        name="rmsnorm",
    )(x, g)


def _proj_a_body(h_ref, w_ref, gq_ref, gkv_ref, wuq_ref, wukv_ref, ct_ref, st_ref,
                 qa_ref, ka_ref, va_ref):
    h = h_ref[...]
    cq = jnp.dot(h, w_ref[:, :Q_LORA_PAD], preferred_element_type=F32)
    cqn = (_rms(cq, Q_LORA) * gq_ref[...]).astype(BF16)
    ckr = jnp.dot(h, w_ref[:, Q_LORA_PAD:], preferred_element_type=F32)
    qa = jnp.dot(cqn, wuq_ref[...], preferred_element_type=F32)
    ckvn = (_rms(ckr[:, :KV_LORA]) * gkv_ref[...]).astype(BF16)
    ct = ct_ref[...]
    st = st_ref[...]
    kpe = _rope(ckr[:, KV_LORA:], ct, st).astype(BF16)
    kva = jnp.dot(ckvn, wukv_ref[...], preferred_element_type=F32)
    for hh in range(A_HEADS):
        c0 = hh * A_QK
        qa_ref[:, c0:c0 + A_NOPE] = (qa[:, c0:c0 + A_NOPE] * SCALE_A).astype(BF16)
        qa_ref[:, c0 + A_NOPE:c0 + A_QK] = (
            _rope(qa[:, c0 + A_NOPE:c0 + A_QK], ct, st) * SCALE_A).astype(BF16)
        ka_ref[:, c0:c0 + A_NOPE] = kva[:, hh * A_NOPE:(hh + 1) * A_NOPE].astype(BF16)
        ka_ref[:, c0 + A_NOPE:c0 + A_QK] = kpe
    va_ref[...] = kva[:, A_HEADS * A_NOPE:].astype(BF16)


def _proj_b_body(h_ref, w_ref, gq_ref, gk_ref, ct_ref, st_ref, q_ref, k_ref, v_ref):
    h = h_ref[...]
    ct = ct_ref[...]
    st = st_ref[...]

    def pair(col):
        return jnp.dot(h, w_ref[:, col:col + 2 * HEAD_DIM], preferred_element_type=F32)

    for i in range(0, B_HEADS, 2):
        pq = pair(i * HEAD_DIM)
        for j in range(2):
            t = _rms(pq[:, j * HEAD_DIM:(j + 1) * HEAD_DIM]) * gq_ref[...]
            q_ref[:, (i + j) * HEAD_DIM:(i + j + 1) * HEAD_DIM] = (_rope(t, ct, st) * SCALE_H).astype(BF16)
    k0 = B_HEADS * HEAD_DIM
    pk = pair(k0)
    for i in range(B_KV_HEADS):
        t = _rms(pk[:, i * HEAD_DIM:(i + 1) * HEAD_DIM]) * gk_ref[...]
        k_ref[:, i * HEAD_DIM:(i + 1) * HEAD_DIM] = _rope(t, ct, st).astype(BF16)
    v_ref[...] = pair(k0 + B_KV_HEADS * HEAD_DIM).astype(BF16)


def _proj_c_body(h_ref, w_ref, ct_ref, st_ref, q_ref, k_ref, v_ref):
    pc = jnp.dot(h_ref[...], w_ref[...], preferred_element_type=F32)
    ct = ct_ref[...]
    st = st_ref[...]
    W = C_HEADS * HEAD_DIM
    for i in range(C_HEADS):
        sl = slice(i * HEAD_DIM, (i + 1) * HEAD_DIM)
        q_ref[:, sl] = (_rope(pc[:, sl], ct, st) * SCALE_H).astype(BF16)
        k_ref[:, sl] = _rope(pc[:, W + i * HEAD_DIM:W + (i + 1) * HEAD_DIM], ct, st).astype(BF16)
    v_ref[...] = pc[:, 2 * W:].astype(BF16)


def _row_spec(tm, w):
    return pl.BlockSpec((tm, w), lambda i: (i, 0))


def _proj_call(body, name, h, consts, layer, tables, row_widths, seq, tm=512):
    T = h.shape[0]
    nseq = seq // tm
    tab_spec = pl.BlockSpec((tm, HEAD_DIM), lambda i: (i % nseq, 0))
    return pl.pallas_call(
        body,
        grid=(T // tm,),
        in_specs=([_row_spec(tm, h.shape[1])] + [_layer_spec(c, layer) for c in consts]
                  + [tab_spec] * len(tables)),
        out_specs=[_row_spec(tm, w) for w in row_widths],
        out_shape=[jax.ShapeDtypeStruct((T, w), BF16) for w in row_widths],
        compiler_params=_params(1),
        name=name,
    )(h, *consts, *tables)


def _attn_body(*refs, dk, dv, tq, tk, nk, nq, reach):
    if reach is None:
        q_ref, k_ref, v_ref, o_ref, s0_ref, s1_ref, m_ref = refs
        bias_ref = None
    else:
        q_ref, k_ref, v_ref, bias_ref, o_ref, s0_ref, s1_ref, m_ref = refs
    s_refs = (s0_ref, s1_ref)
    n_chunks = nk if reach is None else 2 * reach + 1

    def block(idx, size):
        if isinstance(idx, int):
            return slice(idx * size, (idx + 1) * size)
        return pl.ds(pl.multiple_of(idx * size, size), size)

    def chunks_of(t):
        if reach is None:
            return [(block(c, tk), None) for c in range(nk)]
        out = []
        for off in range(-reach, reach + 1):
            j = t + off
            if isinstance(t, int):
                in_range = 0 <= j < nk
                out.append((block(min(max(j, 0), nk - 1), tk), off + reach if in_range else 2 * reach + 1))
            else:
                in_range = jnp.logical_and(j >= 0, j < nk)
                out.append((block(jnp.clip(j, 0, nk - 1), tk), jnp.where(in_range, off + reach, 2 * reach + 1)))
        return out

    def stage(t, parity, do_scores, do_values):
        score_slot, value_slot = parity, 1 - parity
        if do_scores:
            q = q_ref[0, block(t, tq), :]
            cur_chunks = chunks_of(t)
            m_new = jnp.full((1, tq), MASK_VALUE, F32)
        if do_values:
            m = m_ref[value_slot]
            l = jnp.zeros((1, tq), F32)
            acc = jnp.zeros((dv, tq), F32)
            prev_chunks = chunks_of(t - 1)
        for c in range(n_chunks):
            rows = block(c, tk)
            if do_scores:
                kv, bias_idx = cur_chunks[c]
                s = lax.dot_general(k_ref[0, kv, :], q, (((1,), (1,)), ((), ())),
                                    preferred_element_type=F32)
                if bias_idx is not None:
                    s = s + bias_ref[bias_idx]
                s_refs[score_slot][rows, :] = s
                m_new = jnp.maximum(m_new, jnp.max(s, axis=0, keepdims=True))
            if do_values:
                p = jnp.exp2(s_refs[value_slot][rows, :] - m)
                l = l + jnp.sum(p, axis=0, keepdims=True)
                acc = acc + lax.dot_general(v_ref[0, prev_chunks[c][0], :], p.astype(BF16),
                                            (((0,), (0,)), ((), ())),
                                            preferred_element_type=F32)
        if do_scores:
            m_ref[score_slot] = m_new
        if do_values:
            o_ref[0, block(t - 1, tq), :] = (acc / l).T.astype(o_ref.dtype)

    stage(0, 0, True, False)

    def stage_pair(j, carry):
        stage(2 * j + 1, 1, True, True)
        stage(2 * j + 2, 0, True, True)
        return carry

    lax.fori_loop(0, (nq - 2) // 2, stage_pair, 0)
    stage(nq - 1, 1, True, True)
    stage(nq, 0, False, True)


def _attention(q, k, v, *, heads, q_per_kv, dk, dv, tq, tk, bias=None, reach=None, name):
    B, S, _ = q.shape
    nk, nq = S // tk, S // tq
    assert nq >= 2 and nq % 2 == 0, "stages are paired by parity"
    in_specs = [pl.BlockSpec((1, S, dk), lambda b, h: (b, 0, h)),
                pl.BlockSpec((1, S, dk), lambda b, h: (b, 0, h // q_per_kv)),
                pl.BlockSpec((1, S, dv), lambda b, h: (b, 0, h // q_per_kv))]
    args = [q, k, v]
    if bias is not None:
        in_specs.append(pl.BlockSpec(bias.shape, lambda b, h: (0, 0, 0)))
        args.append(bias)
    n_keys = (nk if reach is None else 2 * reach + 1) * tk
    return pl.pallas_call(
        functools.partial(_attn_body, dk=dk, dv=dv, tq=tq, tk=tk, nk=nk, nq=nq, reach=reach),
        grid=(B, heads),
        in_specs=in_specs,
        out_specs=pl.BlockSpec((1, S, dv), lambda b, h: (b, 0, h)),
        out_shape=jax.ShapeDtypeStruct((B, S, heads * dv), BF16),
        scratch_shapes=[pltpu.VMEM((n_keys, tq), F32), pltpu.VMEM((n_keys, tq), F32),
                        pltpu.VMEM((2, 1, tq), F32)],
        compiler_params=_params(2),
        name=name,
    )(*args)


def _dilated_bias():
    off = (np.arange(-C_REACH, C_REACH + 1)[:, None, None] * ATT_BLK
           + np.arange(ATT_BLK)[None, :, None] - np.arange(ATT_BLK)[None, None, :])
    count = np.zeros(off.shape, np.float64)
    for window, dil in C_BRANCHES:
        count += (off % dil == 0) & (np.abs(off) <= (window // (2 * dil)) * dil)
    bias = np.where(count > 0, np.log2(np.maximum(count, 1.0)), MASK_VALUE)
    bias = np.concatenate([bias, np.full((1, ATT_BLK, ATT_BLK), MASK_VALUE)], axis=0)
    return jnp.asarray(bias, F32)


def _out_body(oa_ref, ob_ref, oc_ref, g_ref, w_ref, x_ref, g2_ref, xo_ref, h2_ref, wb_ref):
    @pl.when(pl.program_id(0) == 0)
    def _():
        wb_ref[...] = w_ref[...].astype(BF16)

    mixed = jnp.concatenate([_rms(r[...].astype(F32)) for r in (oa_ref, ob_ref, oc_ref)], axis=1)
    mixed = (mixed * g_ref[...]).astype(BF16)
    y = x_ref[...] + jnp.dot(mixed, wb_ref[...], preferred_element_type=F32)
    xo_ref[...] = y
    h2_ref[...] = (_rms(y) * g2_ref[...]).astype(BF16)


def _out_proj(oa, ob, oc, g_out, w_out, x, ln2_g, layer, tm=512):
    T = x.shape[0]
    return pl.pallas_call(
        _out_body,
        grid=(T // tm,),
        in_specs=[_row_spec(tm, oa.shape[1]), _row_spec(tm, ob.shape[1]), _row_spec(tm, oc.shape[1]),
                  _layer_spec(g_out, layer), _layer_spec(w_out, layer), _row_spec(tm, D_MODEL),
                  _layer_spec(ln2_g, layer)],
        out_specs=[_row_spec(tm, D_MODEL), _row_spec(tm, D_MODEL)],
        out_shape=[jax.ShapeDtypeStruct((T, D_MODEL), F32), jax.ShapeDtypeStruct((T, D_MODEL), BF16)],
        scratch_shapes=[pltpu.VMEM(w_out.shape[1:], BF16)],
        compiler_params=_params(1),
        name="out_proj",
    )(oa, ob, oc, g_out, w_out, x, ln2_g)


def _ff1_body(a_ref, w_ref, o_ref, wb_ref):
    @pl.when(pl.program_id(1) == 0)
    def _():
        wb_ref[...] = w_ref[...].astype(BF16)

    acc = jnp.dot(a_ref[...], wb_ref[...], preferred_element_type=F32)
    o_ref[...] = jnp.square(jnp.maximum(acc, 0.0)).astype(o_ref.dtype)


def _ff1(a, w, layer, tm=1024, tn=1024):
    M, K = a.shape
    N = w.shape[2]
    return pl.pallas_call(
        _ff1_body,
        grid=(N // tn, M // tm),
        in_specs=[pl.BlockSpec((tm, K), lambda j, i: (i, 0)),
                  pl.BlockSpec((None, K, tn), lambda j, i: (layer, 0, j))],
        out_specs=pl.BlockSpec((tm, tn), lambda j, i: (i, j)),
        out_shape=jax.ShapeDtypeStruct((M, N), BF16),
        scratch_shapes=[pltpu.VMEM((K, tn), BF16)],
        compiler_params=_params(2),
        name="ff1",
    )(a, w)


def _ff2_body(a_ref, w_ref, x_ref, o_ref, acc_ref):
    k = pl.program_id(2)
    last = pl.num_programs(2) - 1

    def partial_product():
        return jnp.dot(a_ref[...], w_ref[...], preferred_element_type=F32)

    @pl.when(k == 0)
    def _():
        acc_ref[...] = x_ref[...] + partial_product()

    @pl.when(jnp.logical_and(k > 0, k < last))
    def _():
        acc_ref[...] += partial_product()

    @pl.when(k == last)
    def _():
        o_ref[...] = acc_ref[...] + partial_product()


def _ff2(a, w, x, layer, tm=1024, tn=1024, tk=2048):
    M, K = a.shape
    N = w.shape[2]
    return pl.pallas_call(
        _ff2_body,
        grid=(M // tm, N // tn, K // tk),
        in_specs=[pl.BlockSpec((tm, tk), lambda i, j, k: (i, k)),
                  pl.BlockSpec((None, tk, tn), lambda i, j, k: (layer, k, j)),
                  pl.BlockSpec((tm, tn), lambda i, j, k: (i, j))],
        out_specs=pl.BlockSpec((tm, tn), lambda i, j, k: (i, j)),
        out_shape=jax.ShapeDtypeStruct((M, N), F32),
        scratch_shapes=[pltpu.VMEM((tm, tn), F32)],
        compiler_params=_params(3),
        name="ff2",
    )(a, w, x)


def _rope_tables(S):
    def angles(pos, dim):
        inv = jnp.power(ROPE_THETA, -jnp.arange(0, dim, 2, dtype=F32) / dim)
        ang = pos.astype(F32)[:, None] * inv[None, :]
        return jnp.cos(ang), jnp.sin(ang)

    t = jnp.arange(S)
    cos_p, sin_p = angles(t, A_ROPE)
    cos_r, sin_r = angles(t // GRID_W, HEAD_DIM // 2)
    cos_c, sin_c = angles(t % GRID_W, HEAD_DIM // 2)
    cos_f, sin_f = angles(t, HEAD_DIM)
    ta = (jnp.concatenate([cos_p, cos_p, cos_p, cos_p], 1), jnp.concatenate([-sin_p, sin_p, sin_p, sin_p], 1))
    tb = (jnp.concatenate([cos_r, cos_c, cos_r, cos_c], 1), jnp.concatenate([-sin_r, -sin_c, sin_r, sin_c], 1))
    tc = (jnp.concatenate([cos_f, cos_f], 1), jnp.concatenate([-sin_f, sin_f], 1))
    return ta, tb, tc


def _axial_swap(a, lead):
    q = HEAD_DIM // 4
    return a.reshape(lead + (-1, 2, 2, q)).swapaxes(-3, -2).reshape(lead + (-1,))


def _spread_rope(w):
    z = jnp.zeros(w.shape[:-1] + (A_ROPE // 2,), w.dtype)
    return jnp.concatenate([w[..., :A_ROPE // 2], z, w[..., A_ROPE // 2:], z], axis=-1)


def _prep_weights(w_in, g_q_a, w_uq, g_kv_a, w_ukv, g_qn_b, g_kn_b):
    L = w_in.shape[0]
    sizes = (Q_LORA, KV_LORA, A_ROPE, B_HEADS * HEAD_DIM, B_KV_HEADS * HEAD_DIM, B_KV_HEADS * HEAD_DIM,
             C_HEADS * HEAD_DIM, C_HEADS * HEAD_DIM, C_HEADS * HEAD_DIM)
    offs = np.concatenate([[0], np.cumsum(sizes)])
    w_in = w_in.astype(BF16)
    cols = [w_in[:, :, offs[i]:offs[i + 1]] for i in range(len(sizes))]
    lead = (L, D_MODEL)

    w_a = jnp.concatenate([cols[0], jnp.zeros((L, D_MODEL, Q_LORA_PAD - Q_LORA), BF16),
                           cols[1], _spread_rope(cols[2])], axis=-1)
    w_b = jnp.concatenate([_axial_swap(cols[3], lead), _axial_swap(cols[4], lead), cols[5]], axis=-1)
    w_c = w_in[:, :, offs[6]:]

    uq = w_uq.astype(BF16).reshape(L, Q_LORA, A_HEADS, A_NOPE + A_ROPE)
    uq = jnp.concatenate([uq[..., :A_NOPE], _spread_rope(uq[..., A_NOPE:])], axis=-1)
    uq = uq.reshape(L, Q_LORA, A_HEADS * A_QK)
    uq = jnp.concatenate([uq, jnp.zeros((L, Q_LORA_PAD - Q_LORA, A_HEADS * A_QK), BF16)], axis=1)
    ukv = w_ukv.astype(BF16).reshape(L, KV_LORA, A_HEADS, A_NOPE + A_V)
    ukv = jnp.concatenate([ukv[..., :A_NOPE].reshape(L, KV_LORA, A_HEADS * A_NOPE),
                           ukv[..., A_NOPE:].reshape(L, KV_LORA, A_HEADS * A_V)], axis=-1)
    gq = jnp.concatenate([g_q_a, jnp.zeros((L, Q_LORA_PAD - Q_LORA), g_q_a.dtype)], axis=-1)
    return dict(w_a=w_a, w_b=w_b, w_c=w_c, uq=uq, ukv=ukv, gq=gq[:, None, :], gkv=g_kv_a[:, None, :],
                gqb=_axial_swap(g_qn_b, (L,))[:, None, :], gkb=_axial_swap(g_kn_b, (L,))[:, None, :])


def kernel(x, ln1_g, w_in, g_q_a, w_uq, g_kv_a, w_ukv, g_qn_b, g_kn_b, g_out, w_out, ln2_g, w_ff1, w_ff2, ln_f_g):
    B, S, D = x.shape
    T = B * S
    depth = w_in.shape[0]
    p = _prep_weights(w_in, g_q_a, w_uq, g_kv_a, w_ukv, g_qn_b, g_kn_b)
    w_ff2_b = w_ff2.astype(BF16)
    ln1, ln2, g_o = ln1_g[:, None, :], ln2_g[:, None, :], g_out[:, None, :]
    ta, tb, tc = _rope_tables(S)
    bias_c = _dilated_bias()

    xt = x.reshape(T, D)
    for l in range(depth):
        h = _rmsnorm(xt, ln1, l, BF16)
        qa, ka, va = _proj_call(_proj_a_body, "proj_a", h,
                                 [p["w_a"], p["gq"], p["gkv"], p["uq"], p["ukv"]], l, ta,
                                 (A_HEADS * A_QK, A_HEADS * A_QK, A_HEADS * A_V), S)
        qb, kb, vb = _proj_call(_proj_b_body, "proj_b", h, [p["w_b"], p["gqb"], p["gkb"]], l, tb,
                                 (B_HEADS * HEAD_DIM, B_KV_HEADS * HEAD_DIM, B_KV_HEADS * HEAD_DIM), S)
        qc, kc, vc = _proj_call(_proj_c_body, "proj_c", h, [p["w_c"]], l, tc,
                                 (C_HEADS * HEAD_DIM,) * 3, S)

        def b3(a):
            return a.reshape(B, S, a.shape[-1])

        o_a = _attention(b3(qa), b3(ka), b3(va), heads=A_HEADS, q_per_kv=1, dk=A_QK, dv=A_V,
                         tq=ATT_BLK, tk=ATT_BLK, name="attn_a")
        o_b = _attention(b3(qb), b3(kb), b3(vb), heads=B_HEADS, q_per_kv=B_GROUP, dk=HEAD_DIM, dv=HEAD_DIM,
                         tq=ATT_BLK, tk=ATT_BLK, name="attn_b")
        o_c = _attention(b3(qc), b3(kc), b3(vc), heads=C_HEADS, q_per_kv=1, dk=HEAD_DIM, dv=HEAD_DIM,
                         tq=ATT_BLK, tk=ATT_BLK, bias=bias_c, reach=C_REACH, name="attn_c")

        xt, h2 = _out_proj(o_a.reshape(T, -1), o_b.reshape(T, -1), o_c.reshape(T, -1),
                           g_o, w_out, xt, ln2, l)
        u = _ff1(h2, w_ff1, l)
        xt = _ff2(u, w_ff2_b, xt, l)

    return _rmsnorm(xt, ln_f_g[None, None, :], 0, F32).reshape(B, S, D)
```

```python
import functools
import math

import jax
import jax.numpy as jnp
import numpy as np
from jax import lax
from jax.experimental import pallas as pl
from jax.experimental.pallas import tpu as pltpu

F32 = jnp.float32
BF16 = jnp.bfloat16

D_MODEL = 2048
HEAD_DIM = 128
A_HEADS, A_NOPE, A_ROPE, A_V = 4, 128, 64, 128
Q_LORA, KV_LORA = 448, 512
Q_LORA_PAD = 512
B_HEADS, B_KV_HEADS = 6, 2
B_GROUP = B_HEADS // B_KV_HEADS
GRID_W = 64
C_HEADS = 6
C_BRANCHES = ((128, 1), (512, 4), (2048, 16))
D_FF = 4 * D_MODEL
ROPE_THETA = 10000.0
EPS = 1e-6
MASK_VALUE = -1e30
LOG2E = math.log2(math.e)
SCALE_A = LOG2E / math.sqrt(A_NOPE + A_ROPE)
SCALE_H = LOG2E / math.sqrt(HEAD_DIM)

A_QK = 256
ATT_BLK = 512
C_REACH = 2
VMEM_LIMIT = 56 * 2**20


def _params(n_axes, flags=None):
    return pltpu.CompilerParams(dimension_semantics=("arbitrary",) * n_axes,
                                vmem_limit_bytes=VMEM_LIMIT, flags=flags)


def _rms(t, n=None):
    n = t.shape[-1] if n is None else n
    return t * lax.rsqrt(jnp.sum(t * t, axis=-1, keepdims=True) * (1.0 / n) + EPS)


def _rope(t, ct, st):
    return t * ct + pltpu.roll(t, 64, 1) * st


def _rmsnorm_body(x_ref, g_ref, o_ref):
    o_ref[...] = (_rms(x_ref[...]) * g_ref[...]).astype(o_ref.dtype)


def _layer_spec(stacked, layer):
    index = (layer,) + (0,) * (stacked.ndim - 1)
    return pl.BlockSpec((None,) + stacked.shape[1:], lambda *_: index, pipeline_mode=pl.Buffered(1))


def _rmsnorm(x, g, layer, out_dtype, tm=512):
    T, W = x.shape
    return pl.pallas_call(
        _rmsnorm_body,
        grid=(T // tm,),
        in_specs=[pl.BlockSpec((tm, W), lambda i: (i, 0)), _layer_spec(g, layer)],
        out_specs=pl.BlockSpec((tm, W), lambda i: (i, 0)),
        out_shape=jax.ShapeDtypeStruct((T, W), out_dtype),
        compiler_params=_params(1),
        name="rmsnorm",
    )(x, g)


A_RAW = 2 * A_HEADS * A_QK + HEAD_DIM
B_RAW = (B_HEADS + 2 * B_KV_HEADS) * HEAD_DIM
C_RAW = 3 * C_HEADS * HEAD_DIM


def _proj_a_matmul(ins, raw_ref):
    h_ref, w_ref, gq_ref, gkv_ref, wuq_ref, wukv_ref = ins[:6]
    h = h_ref[...]
    cq = jnp.dot(h, w_ref[:, :Q_LORA_PAD], preferred_element_type=F32)
    cqn = (_rms(cq, Q_LORA) * gq_ref[...]).astype(BF16)
    ckr = jnp.dot(h, w_ref[:, Q_LORA_PAD:], preferred_element_type=F32)
    ckvn = (_rms(ckr[:, :KV_LORA]) * gkv_ref[...]).astype(BF16)
    n = A_HEADS * A_QK
    raw_ref[:, :n] = jnp.dot(cqn, wuq_ref[...], preferred_element_type=F32)
    raw_ref[:, n:2 * n] = jnp.dot(ckvn, wukv_ref[...], preferred_element_type=F32)
    raw_ref[:, 2 * n:] = ckr[:, KV_LORA:]


def _proj_a_epilogue(ins, raw_ref, outs):
    ct, st = ins[6][...], ins[7][...]
    qa_ref, ka_ref, vat_ref = outs
    n = A_HEADS * A_QK
    kpe = _rope(raw_ref[:, 2 * n:], ct, st).astype(BF16)
    for hh in range(A_HEADS):
        c0 = hh * A_QK
        qa_ref[:, c0:c0 + A_NOPE] = (raw_ref[:, c0:c0 + A_NOPE] * SCALE_A).astype(BF16)
        qa_ref[:, c0 + A_NOPE:c0 + A_QK] = (
            _rope(raw_ref[:, c0 + A_NOPE:c0 + A_QK], ct, st) * SCALE_A).astype(BF16)
        ka_ref[:, c0:c0 + A_NOPE] = raw_ref[:, n + hh * A_NOPE:n + (hh + 1) * A_NOPE].astype(BF16)
        ka_ref[:, c0 + A_NOPE:c0 + A_QK] = kpe
    vat_ref[0] = raw_ref[:, n + A_HEADS * A_NOPE:2 * n].T.astype(BF16)


def _proj_b_matmul(ins, raw_ref):
    raw_ref[...] = jnp.dot(ins[0][...], ins[1][...], preferred_element_type=F32)


def _proj_b_epilogue(ins, raw_ref, outs):
    gq_ref, gk_ref = ins[2], ins[3]
    ct, st = ins[4][...], ins[5][...]
    q_ref, k_ref, vt_ref = outs
    for j in range(B_HEADS):
        t = _rms(raw_ref[:, j * HEAD_DIM:(j + 1) * HEAD_DIM]) * gq_ref[...]
        q_ref[:, j * HEAD_DIM:(j + 1) * HEAD_DIM] = (_rope(t, ct, st) * SCALE_H).astype(BF16)
    k0 = B_HEADS * HEAD_DIM
    for j in range(B_KV_HEADS):
        t = _rms(raw_ref[:, k0 + j * HEAD_DIM:k0 + (j + 1) * HEAD_DIM]) * gk_ref[...]
        k_ref[:, j * HEAD_DIM:(j + 1) * HEAD_DIM] = _rope(t, ct, st).astype(BF16)
    vt_ref[0] = raw_ref[:, k0 + B_KV_HEADS * HEAD_DIM:].T.astype(BF16)


def _proj_c_matmul(ins, raw_ref):
    raw_ref[...] = jnp.dot(ins[0][...], ins[1][...], preferred_element_type=F32)


def _proj_c_epilogue(ins, raw_ref, outs):
    ct, st = ins[2][...], ins[3][...]
    q_ref, k_ref, vt_ref = outs
    W = C_HEADS * HEAD_DIM
    for j in range(C_HEADS):
        sl = slice(j * HEAD_DIM, (j + 1) * HEAD_DIM)
        q_ref[:, sl] = (_rope(raw_ref[:, sl], ct, st) * SCALE_H).astype(BF16)
        k_ref[:, sl] = _rope(raw_ref[:, W + j * HEAD_DIM:W + (j + 1) * HEAD_DIM], ct, st).astype(BF16)
    vt_ref[0] = raw_ref[:, 2 * W:].T.astype(BF16)


def _proj_body(*refs, n_in, n_tiles, matmul, epilogue):
    ins, outs, raws = refs[:n_in], refs[n_in:-2], refs[-2:]
    i = pl.program_id(0)

    @pl.when(i == 0)
    def _():
        matmul(ins, raws[0])

    for parity in (0, 1):
        @pl.when(jnp.logical_and(jnp.logical_and(i > 0, i < n_tiles), i % 2 == parity))
        def _():
            matmul(ins, raws[parity])
            epilogue(ins, raws[1 - parity], outs)

    @pl.when(i == n_tiles)
    def _():
        epilogue(ins, raws[(n_tiles - 1) % 2], outs)


def _row_spec(tm, w):
    return pl.BlockSpec((tm, w), lambda i: (i, 0))


def _proj_call(matmul, epilogue, name, h, consts, layer, tables, row_widths, vt_width, n_raw, batch, seq,
               tm=512):
    T = h.shape[0]
    n_tiles, nseq = T // tm, seq // tm

    def prev(i):
        return jnp.maximum(i - 1, 0)

    tab_spec = pl.BlockSpec((tm, HEAD_DIM), lambda i: (prev(i) % nseq, 0))
    n_in = 1 + len(consts) + len(tables)
    return pl.pallas_call(
        functools.partial(_proj_body, n_in=n_in, n_tiles=n_tiles, matmul=matmul, epilogue=epilogue),
        grid=(n_tiles + 1,),
        in_specs=([pl.BlockSpec((tm, h.shape[1]), lambda i: (jnp.minimum(i, n_tiles - 1), 0))]
                  + [_layer_spec(c, layer) for c in consts] + [tab_spec] * len(tables)),
        out_specs=([pl.BlockSpec((tm, w), lambda i: (prev(i), 0)) for w in row_widths]
                   + [pl.BlockSpec((1, vt_width, tm), lambda i: (prev(i) // nseq, 0, prev(i) % nseq))]),
        out_shape=([jax.ShapeDtypeStruct((T, w), BF16) for w in row_widths]
                   + [jax.ShapeDtypeStruct((batch, vt_width, seq), BF16)]),
        scratch_shapes=[pltpu.VMEM((tm, n_raw), F32), pltpu.VMEM((tm, n_raw), F32)],
        compiler_params=_params(1),
        name=name,
    )(h, *consts, *tables)


def _attn_body(*refs, dk, dv, tq, tk, nk, nq, reach):
    if reach is None:
        q_ref, k_ref, vt_ref, o_ref, s0_ref, s1_ref, m_ref = refs
        bias_ref = None
    else:
        q_ref, k_ref, vt_ref, bias_ref, o_ref, s0_ref, s1_ref, m_ref = refs
    s_refs = (s0_ref, s1_ref)
    n_chunks = nk if reach is None else 2 * reach + 1

    def block(idx, size):
        if isinstance(idx, int):
            return slice(idx * size, (idx + 1) * size)
        return pl.ds(pl.multiple_of(idx * size, size), size)

    def chunks_of(t):
        if reach is None:
            return [(block(c, tk), None) for c in range(nk)]
        out = []
        for off in range(-reach, reach + 1):
            j = t + off
            if isinstance(t, int):
                in_range = 0 <= j < nk
                out.append((block(min(max(j, 0), nk - 1), tk), off + reach if in_range else 2 * reach + 1))
            else:
                in_range = jnp.logical_and(j >= 0, j < nk)
                out.append((block(jnp.clip(j, 0, nk - 1), tk), jnp.where(in_range, off + reach, 2 * reach + 1)))
        return out

    def stage(t, parity, do_scores, do_values):
        score_slot, value_slot = parity, 1 - parity
        if do_scores:
            q = q_ref[0, block(t, tq), :]
            cur_chunks = chunks_of(t)
            m_new = jnp.full((1, tq), MASK_VALUE, F32)
        if do_values:
            m = m_ref[value_slot]
            l = jnp.zeros((1, tq), F32)
            acc = jnp.zeros((dv, tq), F32)
            prev_chunks = chunks_of(t - 1)
        for c in range(n_chunks):
            rows = block(c, tk)
            if do_scores:
                kv, bias_idx = cur_chunks[c]
                s = lax.dot_general(k_ref[0, kv, :], q, (((1,), (1,)), ((), ())),
                                    preferred_element_type=F32)
                if bias_idx is not None:
                    s = s + bias_ref[bias_idx]
                s_refs[score_slot][rows, :] = s
                m_new = jnp.maximum(m_new, jnp.max(s, axis=0, keepdims=True))
            if do_values:
                p = jnp.exp2(s_refs[value_slot][rows, :] - m)
                l = l + jnp.sum(p, axis=0, keepdims=True)
                acc = acc + jnp.dot(vt_ref[0, :, prev_chunks[c][0]], p.astype(BF16),
                                    preferred_element_type=F32)
        if do_scores:
            m_ref[score_slot] = m_new
        if do_values:
            o_ref[0, block(t - 1, tq), :] = (acc / l).T.astype(o_ref.dtype)

    stage(0, 0, True, False)

    def stage_pair(j, carry):
        stage(2 * j + 1, 1, True, True)
        stage(2 * j + 2, 0, True, True)
        return carry

    lax.fori_loop(0, (nq - 2) // 2, stage_pair, 0)
    stage(nq - 1, 1, True, True)
    stage(nq, 0, False, True)


def _attention(q, k, vt, *, heads, q_per_kv, dk, dv, tq, tk, bias=None, reach=None, name):
    B, S, _ = q.shape
    nk, nq = S // tk, S // tq
    assert nq >= 2 and nq % 2 == 0, "stages are paired by parity"
    in_specs = [pl.BlockSpec((1, S, dk), lambda b, h: (b, 0, h)),
                pl.BlockSpec((1, S, dk), lambda b, h: (b, 0, h // q_per_kv)),
                pl.BlockSpec((1, dv, S), lambda b, h: (b, h // q_per_kv, 0))]
    args = [q, k, vt]
    if bias is not None:
        in_specs.append(pl.BlockSpec(bias.shape, lambda b, h: (0, 0, 0)))
        args.append(bias)
    n_keys = (nk if reach is None else 2 * reach + 1) * tk
    return pl.pallas_call(
        functools.partial(_attn_body, dk=dk, dv=dv, tq=tq, tk=tk, nk=nk, nq=nq, reach=reach),
        grid=(B, heads),
        in_specs=in_specs,
        out_specs=pl.BlockSpec((1, S, dv), lambda b, h: (b, 0, h)),
        out_shape=jax.ShapeDtypeStruct((B, S, heads * dv), BF16),
        scratch_shapes=[pltpu.VMEM((n_keys, tq), F32), pltpu.VMEM((n_keys, tq), F32),
                        pltpu.VMEM((2, 1, tq), F32)],
        compiler_params=_params(2),
        name=name,
    )(*args)


def _dilated_bias():
    off = (np.arange(-C_REACH, C_REACH + 1)[:, None, None] * ATT_BLK
           + np.arange(ATT_BLK)[None, :, None] - np.arange(ATT_BLK)[None, None, :])
    count = np.zeros(off.shape, np.float64)
    for window, dil in C_BRANCHES:
        count += (off % dil == 0) & (np.abs(off) <= (window // (2 * dil)) * dil)
    bias = np.where(count > 0, np.log2(np.maximum(count, 1.0)), MASK_VALUE)
    bias = np.concatenate([bias, np.full((1, ATT_BLK, ATT_BLK), MASK_VALUE)], axis=0)
    return jnp.asarray(bias, F32)


def _out_body(oa_ref, ob_ref, oc_ref, g_ref, w_ref, x_ref, g2_ref, xo_ref, h2_ref, wb_ref):
    @pl.when(pl.program_id(0) == 0)
    def _():
        wb_ref[...] = w_ref[...].astype(BF16)

    mixed = jnp.concatenate([_rms(r[...].astype(F32)) for r in (oa_ref, ob_ref, oc_ref)], axis=1)
    mixed = (mixed * g_ref[...]).astype(BF16)
    y = x_ref[...] + jnp.dot(mixed, wb_ref[...], preferred_element_type=F32)
    xo_ref[...] = y
    h2_ref[...] = (_rms(y) * g2_ref[...]).astype(BF16)


def _out_proj(oa, ob, oc, g_out, w_out, x, ln2_g, layer, tm=512):
    T = x.shape[0]
    return pl.pallas_call(
        _out_body,
        grid=(T // tm,),
        in_specs=[_row_spec(tm, oa.shape[1]), _row_spec(tm, ob.shape[1]), _row_spec(tm, oc.shape[1]),
                  _layer_spec(g_out, layer), _layer_spec(w_out, layer), _row_spec(tm, D_MODEL),
                  _layer_spec(ln2_g, layer)],
        out_specs=[_row_spec(tm, D_MODEL), _row_spec(tm, D_MODEL)],
        out_shape=[jax.ShapeDtypeStruct((T, D_MODEL), F32), jax.ShapeDtypeStruct((T, D_MODEL), BF16)],
        scratch_shapes=[pltpu.VMEM(w_out.shape[1:], BF16)],
        compiler_params=_params(1),
        name="out_proj",
    )(oa, ob, oc, g_out, w_out, x, ln2_g)


def _ff1_body(a_ref, w_ref, o_ref, wb_ref):
    @pl.when(pl.program_id(1) == 0)
    def _():
        wb_ref[...] = w_ref[...].astype(BF16)

    acc = jnp.dot(a_ref[...], wb_ref[...], preferred_element_type=F32)
    o_ref[...] = jnp.square(jnp.maximum(acc, 0.0)).astype(o_ref.dtype)


def _ff1(a, w, layer, tm=1024, tn=1024):
    M, K = a.shape
    N = w.shape[2]
    return pl.pallas_call(
        _ff1_body,
        grid=(N // tn, M // tm),
        in_specs=[pl.BlockSpec((tm, K), lambda j, i: (i, 0)),
                  pl.BlockSpec((None, K, tn), lambda j, i: (layer, 0, j))],
        out_specs=pl.BlockSpec((tm, tn), lambda j, i: (i, j)),
        out_shape=jax.ShapeDtypeStruct((M, N), BF16),
        scratch_shapes=[pltpu.VMEM((K, tn), BF16)],
        compiler_params=_params(2),
        name="ff1",
    )(a, w)


def _ff2_body(a_ref, w_ref, x_ref, o_ref, acc_ref):
    k = pl.program_id(2)
    last = pl.num_programs(2) - 1

    def partial_product():
        return jnp.dot(a_ref[...], w_ref[...], preferred_element_type=F32)

    @pl.when(k == 0)
    def _():
        acc_ref[...] = x_ref[...] + partial_product()

    @pl.when(jnp.logical_and(k > 0, k < last))
    def _():
        acc_ref[...] += partial_product()

    @pl.when(k == last)
    def _():
        o_ref[...] = acc_ref[...] + partial_product()


def _ff2(a, w, x, layer, tm=1024, tn=1024, tk=2048):
    M, K = a.shape
    N = w.shape[2]
    return pl.pallas_call(
        _ff2_body,
        grid=(M // tm, N // tn, K // tk),
        in_specs=[pl.BlockSpec((tm, tk), lambda i, j, k: (i, k)),
                  pl.BlockSpec((None, tk, tn), lambda i, j, k: (layer, k, j)),
                  pl.BlockSpec((tm, tn), lambda i, j, k: (i, j))],
        out_specs=pl.BlockSpec((tm, tn), lambda i, j, k: (i, j)),
        out_shape=jax.ShapeDtypeStruct((M, N), F32),
        scratch_shapes=[pltpu.VMEM((tm, tn), F32)],
        compiler_params=_params(3),
        name="ff2",
    )(a, w, x)


def _rope_tables(S):
    def angles(pos, dim):
        inv = jnp.power(ROPE_THETA, -jnp.arange(0, dim, 2, dtype=F32) / dim)
        ang = pos.astype(F32)[:, None] * inv[None, :]
        return jnp.cos(ang), jnp.sin(ang)

    t = jnp.arange(S)
    cos_p, sin_p = angles(t, A_ROPE)
    cos_r, sin_r = angles(t // GRID_W, HEAD_DIM // 2)
    cos_c, sin_c = angles(t % GRID_W, HEAD_DIM // 2)
    cos_f, sin_f = angles(t, HEAD_DIM)
    ta = (jnp.concatenate([cos_p, cos_p, cos_p, cos_p], 1), jnp.concatenate([-sin_p, sin_p, sin_p, sin_p], 1))
    tb = (jnp.concatenate([cos_r, cos_c, cos_r, cos_c], 1), jnp.concatenate([-sin_r, -sin_c, sin_r, sin_c], 1))
    tc = (jnp.concatenate([cos_f, cos_f], 1), jnp.concatenate([-sin_f, sin_f], 1))
    return ta, tb, tc


def _axial_swap(a, lead):
    q = HEAD_DIM // 4
    return a.reshape(lead + (-1, 2, 2, q)).swapaxes(-3, -2).reshape(lead + (-1,))


def _spread_rope(w):
    z = jnp.zeros(w.shape[:-1] + (A_ROPE // 2,), w.dtype)
    return jnp.concatenate([w[..., :A_ROPE // 2], z, w[..., A_ROPE // 2:], z], axis=-1)


def _prep_weights(w_in, g_q_a, w_uq, g_kv_a, w_ukv, g_qn_b, g_kn_b):
    L = w_in.shape[0]
    sizes = (Q_LORA, KV_LORA, A_ROPE, B_HEADS * HEAD_DIM, B_KV_HEADS * HEAD_DIM, B_KV_HEADS * HEAD_DIM,
             C_HEADS * HEAD_DIM, C_HEADS * HEAD_DIM, C_HEADS * HEAD_DIM)
    offs = np.concatenate([[0], np.cumsum(sizes)])
    w_in = w_in.astype(BF16)
    cols = [w_in[:, :, offs[i]:offs[i + 1]] for i in range(len(sizes))]
    lead = (L, D_MODEL)

    w_a = jnp.concatenate([cols[0], jnp.zeros((L, D_MODEL, Q_LORA_PAD - Q_LORA), BF16),
                           cols[1], _spread_rope(cols[2])], axis=-1)
    w_b = jnp.concatenate([_axial_swap(cols[3], lead), _axial_swap(cols[4], lead), cols[5]], axis=-1)
    w_c = w_in[:, :, offs[6]:]

    uq = w_uq.astype(BF16).reshape(L, Q_LORA, A_HEADS, A_NOPE + A_ROPE)
    uq = jnp.concatenate([uq[..., :A_NOPE], _spread_rope(uq[..., A_NOPE:])], axis=-1)
    uq = uq.reshape(L, Q_LORA, A_HEADS * A_QK)
    uq = jnp.concatenate([uq, jnp.zeros((L, Q_LORA_PAD - Q_LORA, A_HEADS * A_QK), BF16)], axis=1)
    ukv = w_ukv.astype(BF16).reshape(L, KV_LORA, A_HEADS, A_NOPE + A_V)
    ukv = jnp.concatenate([ukv[..., :A_NOPE].reshape(L, KV_LORA, A_HEADS * A_NOPE),
                           ukv[..., A_NOPE:].reshape(L, KV_LORA, A_HEADS * A_V)], axis=-1)
    gq = jnp.concatenate([g_q_a, jnp.zeros((L, Q_LORA_PAD - Q_LORA), g_q_a.dtype)], axis=-1)
    return dict(w_a=w_a, w_b=w_b, w_c=w_c, uq=uq, ukv=ukv, gq=gq[:, None, :], gkv=g_kv_a[:, None, :],
                gqb=_axial_swap(g_qn_b, (L,))[:, None, :], gkb=_axial_swap(g_kn_b, (L,))[:, None, :])


def kernel(x, ln1_g, w_in, g_q_a, w_uq, g_kv_a, w_ukv, g_qn_b, g_kn_b, g_out, w_out, ln2_g, w_ff1, w_ff2, ln_f_g):
    B, S, D = x.shape
    T = B * S
    depth = w_in.shape[0]
    p = _prep_weights(w_in, g_q_a, w_uq, g_kv_a, w_ukv, g_qn_b, g_kn_b)
    w_ff2_b = w_ff2.astype(BF16)
    ln1, ln2, g_o = ln1_g[:, None, :], ln2_g[:, None, :], g_out[:, None, :]
    ta, tb, tc = _rope_tables(S)
    bias_c = _dilated_bias()

    xt = x.reshape(T, D)
    for l in range(depth):
        h = _rmsnorm(xt, ln1, l, BF16)
        qa, ka, vat = _proj_call(_proj_a_matmul, _proj_a_epilogue, "proj_a", h,
                                 [p["w_a"], p["gq"], p["gkv"], p["uq"], p["ukv"]], l, ta,
                                 (A_HEADS * A_QK, A_HEADS * A_QK), A_HEADS * A_V, A_RAW, B, S)
        qb, kb, vbt = _proj_call(_proj_b_matmul, _proj_b_epilogue, "proj_b", h,
                                 [p["w_b"], p["gqb"], p["gkb"]], l, tb,
                                 (B_HEADS * HEAD_DIM, B_KV_HEADS * HEAD_DIM), B_KV_HEADS * HEAD_DIM, B_RAW, B, S)
        qc, kc, vct = _proj_call(_proj_c_matmul, _proj_c_epilogue, "proj_c", h, [p["w_c"]], l, tc,
                                 (C_HEADS * HEAD_DIM,) * 2, C_HEADS * HEAD_DIM, C_RAW, B, S)

        def b3(a):
            return a.reshape(B, S, a.shape[-1])

        o_a = _attention(b3(qa), b3(ka), vat, heads=A_HEADS, q_per_kv=1, dk=A_QK, dv=A_V,
                         tq=ATT_BLK, tk=ATT_BLK, name="attn_a")
        o_b = _attention(b3(qb), b3(kb), vbt, heads=B_HEADS, q_per_kv=B_GROUP, dk=HEAD_DIM, dv=HEAD_DIM,
                         tq=ATT_BLK, tk=ATT_BLK, name="attn_b")
        o_c = _attention(b3(qc), b3(kc), vct, heads=C_HEADS, q_per_kv=1, dk=HEAD_DIM, dv=HEAD_DIM,
                         tq=ATT_BLK, tk=ATT_BLK, bias=bias_c, reach=C_REACH, name="attn_c")

        xt, h2 = _out_proj(o_a.reshape(T, -1), o_b.reshape(T, -1), o_c.reshape(T, -1),
                           g_o, w_out, xt, ln2, l)
        u = _ff1(h2, w_ff1, l)
        xt = _ff2(u, w_ff2_b, xt, l)

    return _rmsnorm(xt, ln_f_g[None, None, :], 0, F32).reshape(B, S, D)
```
